```python
import jax
import jax.numpy as jnp
from jax import lax
import numpy as np

D_MODEL = 1024
BATCH = 8
SEQ = 2048
DEPTH = 2
DEC_BATCH = 32
DEC_SEQ = 1
PAST_LEN = 8192
PAGE_SIZE = 128

BRANCH_W = D_MODEL // 2
HEAD_DIM = 64
ROPE_THETA = 500000.0
EPS = 1e-6
NEG = -1e30
FORCE = 1e9
Q_BLOCK = 128

HG_HEADS = 4
HG_K = BRANCH_W // HG_HEADS
HG_V = BRANCH_W // HG_HEADS
HG_CHUNK = 64

DSA_HEADS = BRANCH_W // HEAD_DIM
DSA_KV_HEADS = 2
IDX_HEADS = 8
IDX_DIM = 64
DSA_TOPK = 256

NSA_HEADS = BRANCH_W // HEAD_DIM
NSA_KV_HEADS = 2
CMP_BLOCK = 32
CMP_STRIDE = 16
CMP_HIDDEN = HEAD_DIM
SEL_BLOCK = 64
SEL_TOPN = 16
WINDOW = 512

DSA_KV_W = DSA_KV_HEADS * HEAD_DIM
NSA_KV_W = NSA_KV_HEADS * HEAD_DIM
IN_SPLITS = (
    HG_HEADS * HG_K, HG_HEADS * HG_K, HG_HEADS * HG_V, BRANCH_W,
    DSA_HEADS * HEAD_DIM, DSA_KV_W, DSA_KV_W, IDX_HEADS * IDX_DIM, IDX_DIM, IDX_HEADS, BRANCH_W,
    NSA_HEADS * HEAD_DIM, NSA_KV_W, NSA_KV_W, NSA_KV_W, NSA_KV_W, NSA_KV_W, NSA_KV_W, NSA_HEADS * 3, BRANCH_W,
    D_MODEL, D_MODEL, D_MODEL,
)
IN_COLS = sum(IN_SPLITS)
CACHE_SPLITS = (DSA_KV_W, DSA_KV_W, IDX_DIM, NSA_KV_W, NSA_KV_W, NSA_KV_W, NSA_KV_W)
CACHE_W = sum(CACHE_SPLITS)
WIN_W = 2 * NSA_KV_W

kernel_name = 'hybrid_hgrn2_dsa_nsa_step'


def _split(z, widths):
    cuts = [int(c) for c in np.cumsum(widths)[:-1]]
    return jnp.split(z, cuts, axis=-1)


def _rms(x, g):
    x32 = x.astype(jnp.float32)
    y = x32 * lax.rsqrt(jnp.mean(x32 * x32, axis=-1, keepdims=True) + EPS)
    return (y * g.astype(jnp.float32)).astype(x.dtype)


def _rope(x, pos):
    half = (x.shape[-1] // 4) // 2
    inv = ROPE_THETA ** (-jnp.arange(half, dtype=jnp.float32) / half)
    ang = pos.astype(jnp.float32)[:, None] * inv[None, :]
    cos = jnp.cos(ang)[:, None, :].astype(x.dtype)
    sin = jnp.sin(ang)[:, None, :].astype(x.dtype)
    x1, x2, xp = x[..., :half], x[..., half:2 * half], x[..., 2 * half:]
    return jnp.concatenate([x1 * cos - x2 * sin, x1 * sin + x2 * cos, xp], axis=-1)


def _map_query_blocks(fn, qpos, *arrs):
    lq = qpos.shape[0]
    qb = min(Q_BLOCK, lq)
    pad = (-lq) % qb
    nb = (lq + pad) // qb
    qpos_b = jnp.pad(qpos, (0, pad), mode='edge').reshape(nb, qb)
    blocks = []
    for a in arrs:
        a = jnp.pad(a, [(0, 0), (0, pad)] + [(0, 0)] * (a.ndim - 2), mode='edge')
        blocks.append(jnp.moveaxis(a.reshape(a.shape[0], nb, qb, *a.shape[2:]), 1, 0))
    out = lax.map(lambda args: fn(*args), (qpos_b, *blocks))
    out = jnp.moveaxis(out, 0, 1)
    return out.reshape(out.shape[0], nb * qb, *out.shape[3:])[:, :lq]


def _hgrn2(q, k, v, logf, s0):
    B, L, H, _ = q.shape
    c = min(HG_CHUNK, L)
    pad = (-L) % c
    n = (L + pad) // c

    def chunks(a):
        a = jnp.pad(a.astype(jnp.float32), ((0, 0), (0, pad), (0, 0), (0, 0)))
        return a.reshape(B, n, c, H, a.shape[-1]).transpose(1, 0, 3, 2, 4)

    tril = jnp.tril(jnp.ones((c, c), dtype=bool))

    def step(S, inp):
        qc, kc, vc, gc = inp
        b = jnp.cumsum(gc, axis=2)
        diff = b[:, :, :, None, :] - b[:, :, None, :, :]
        decay = jnp.exp(jnp.where(tril[:, :, None], diff, -jnp.inf))
        A = jnp.einsum('bhtk,bhsk,bhtsk->bhts', qc, kc, decay)
        o = jnp.einsum('bhts,bhsv->bhtv', A, vc) + jnp.einsum('bhtk,bhkv->bhtv', qc * jnp.exp(b), S)
        bl = b[:, :, -1:, :]
        S = jnp.exp(bl[:, :, 0, :, None]) * S + jnp.einsum('bhsk,bhsv->bhkv', kc * jnp.exp(bl - b), vc)
        return S, o

    S, o = lax.scan(step, s0.astype(jnp.float32), (chunks(q), chunks(k), chunks(v), chunks(logf)))
    o = o.transpose(1, 0, 3, 2, 4).reshape(B, n * c, H, v.shape[-1])[:, :L]
    return o, S


def _dsa_attend(q, iq, iw, qpos, k, v, ik):
    lk = k.shape[1]
    n_top = min(DSA_TOPK, lk // 4)
    kpos = jnp.arange(lk, dtype=jnp.int32)
    g = DSA_HEADS // DSA_KV_HEADS
    take = jax.vmap(lambda a, i: a[i])

    def block(qp, qb, iqb, iwb):
        bq, t = qb.shape[:2]
        dots = jnp.einsum('bthd,bsd->bths', iqb, ik, preferred_element_type=jnp.float32) * IDX_DIM ** -0.5
        score = jnp.einsum('bth,bths->bts', iwb.astype(jnp.float32), jax.nn.relu(dots))
        score = jnp.where(kpos[None, None, :] <= qp[None, :, None], score, NEG)
        sel = lax.top_k(score, n_top)[1]
        kg, vg = take(k, sel), take(v, sel)
        qg = qb.reshape(bq, t, DSA_KV_HEADS, g, HEAD_DIM)
        s = jnp.einsum('btkgd,btjkd->btkgj', qg, kg, preferred_element_type=jnp.float32) * HEAD_DIM ** -0.5
        s = jnp.where((sel <= qp[None, :, None])[:, :, None, None, :], s, NEG)
        o = jnp.einsum('btkgj,btjkd->btkgd', jax.nn.softmax(s, axis=-1), vg.astype(jnp.float32))
        return o.reshape(bq, t, DSA_HEADS * HEAD_DIM)

    return _map_query_blocks(block, qpos, q, iq, iw)


def _compress(kv, pe, w1, b1, w2):
    B, lk, h, d = kv.shape
    n_cmp = (lk - CMP_BLOCK) // CMP_STRIDE + 1
    idx = jnp.arange(n_cmp)[:, None] * CMP_STRIDE + jnp.arange(CMP_BLOCK)[None, :]
    blk = kv[:, idx] + pe[:, None, :]
    blk = jnp.swapaxes(blk, 2, 3).reshape(B, n_cmp, h, CMP_BLOCK * d)
    return jax.nn.silu(blk @ w1 + b1) @ w2


def _nsa_cmp_sel(q, qpos, ck, cv, sk, sv):
    B, lk = sk.shape[:2]
    n_cmp = ck.shape[1]
    n_sel = -(-lk // SEL_BLOCK)
    n_top = min(SEL_TOPN, n_sel)
    g = NSA_HEADS // NSA_KV_HEADS
    scale = HEAD_DIM ** -0.5
    c_start = jnp.arange(n_cmp, dtype=jnp.int32) * CMP_STRIDE
    c_end = c_start + CMP_BLOCK - 1
    s_start = jnp.arange(n_sel, dtype=jnp.int32) * SEL_BLOCK
    cover = jnp.clip(jnp.minimum(c_start[:, None] + CMP_BLOCK, s_start[None, :] + SEL_BLOCK)
                     - jnp.maximum(c_start[:, None], s_start[None, :]), 0, None).astype(jnp.float32) / CMP_BLOCK
    padk = ((0, 0), (0, n_sel * SEL_BLOCK - lk), (0, 0), (0, 0))
    skb = jnp.pad(sk, padk).reshape(B, n_sel, SEL_BLOCK, NSA_KV_HEADS, HEAD_DIM).transpose(0, 3, 1, 2, 4)
    svb = jnp.pad(sv, padk).reshape(B, n_sel, SEL_BLOCK, NSA_KV_HEADS, HEAD_DIM).transpose(0, 3, 1, 2, 4)
    heads_idx = jnp.arange(NSA_KV_HEADS)[None, :, None]
    take_blocks = jax.vmap(lambda a, i: a[heads_idx, i])

    def block(qp, qb):
        bq, t = qb.shape[:2]
        qg = qb.reshape(bq, t, NSA_KV_HEADS, g, HEAD_DIM)
        cvalid = (c_end[None, :] <= qp[:, None])[None, :, None, None, :]
        s = jnp.einsum('btkgd,bckd->btkgc', qg, ck, preferred_element_type=jnp.float32) * scale
        p = jnp.where(cvalid, jax.nn.softmax(jnp.where(cvalid, s, NEG), axis=-1), 0.0)
        o_cmp = jnp.einsum('btkgc,bckd->btkgd', p, cv.astype(jnp.float32))
        imp = jnp.einsum('btkgc,cj->btkj', p, cover)
        tb = (qp // SEL_BLOCK)[:, None]
        blk = jnp.arange(n_sel, dtype=jnp.int32)[None, :]
        forced = ((blk == 0) | (blk == tb) | (blk == tb - 1))[None, :, None, :]
        causal = (s_start[None, :] <= qp[:, None])[None, :, None, :]
        imp = jnp.where(causal, jnp.where(forced, FORCE, imp), NEG)
        sel = lax.top_k(imp, n_top)[1]
        kg = take_blocks(skb, sel).reshape(bq, t, NSA_KV_HEADS, n_top * SEL_BLOCK, HEAD_DIM)
        vg = take_blocks(svb, sel).reshape(bq, t, NSA_KV_HEADS, n_top * SEL_BLOCK, HEAD_DIM)
        kpos = (sel[..., None] * SEL_BLOCK + jnp.arange(SEL_BLOCK, dtype=jnp.int32)).reshape(
            bq, t, NSA_KV_HEADS, n_top * SEL_BLOCK)
        s2 = jnp.einsum('btkgd,btkjd->btkgj', qg, kg, preferred_element_type=jnp.float32) * scale
        s2 = jnp.where((kpos <= qp[None, :, None, None])[:, :, :, None, :], s2, NEG)
        o_sel = jnp.einsum('btkgj,btkjd->btkgd', jax.nn.softmax(s2, axis=-1), vg.astype(jnp.float32))
        return jnp.stack([o_cmp, o_sel], axis=2).reshape(bq, t, 2, NSA_HEADS, HEAD_DIM)

    return _map_query_blocks(block, qpos, q)


def _window_attend(q, qpos, k, v, kpos0):
    qb_len = min(Q_BLOCK, qpos.shape[0])
    band = WINDOW + qb_len
    padw = ((0, 0), (WINDOW, qb_len), (0, 0), (0, 0))
    kp, vp = jnp.pad(k, padw), jnp.pad(v, padw)
    g = NSA_HEADS // NSA_KV_HEADS

    def block(qp, qb):
        bq, t = qb.shape[:2]
        start = qp[0] - kpos0
        kb = lax.dynamic_slice_in_dim(kp, start, band, axis=1)
        vb = lax.dynamic_slice_in_dim(vp, start, band, axis=1)
        pos = qp[0] - WINDOW + jnp.arange(band, dtype=jnp.int32)
        valid = (pos[None, :] >= kpos0) & (pos[None, :] <= qp[:, None]) & (qp[:, None] - pos[None, :] < WINDOW)
        qg = qb.reshape(bq, t, NSA_KV_HEADS, g, HEAD_DIM)
        s = jnp.einsum('btkgd,bskd->btkgs', qg, kb, preferred_element_type=jnp.float32) * HEAD_DIM ** -0.5
        s = jnp.where(valid[None, :, None, None, :], s, NEG)
        o = jnp.einsum('btkgs,bskd->btkgd', jax.nn.softmax(s, axis=-1), vb.astype(jnp.float32))
        return o.reshape(bq, t, NSA_HEADS, HEAD_DIM)

    return _map_query_blocks(block, qpos, q)


def _layer(x, pos0, past_rows, win_rows, s0, lb, norm_g, w_in, hg_onorm_g, dsa_qk_g, idx_k_g,
           nsa_qk_g, cmp_pe, cmp_w1, cmp_b1, cmp_w2, w_branch, w_out):
    B, L, _ = x.shape
    dt = x.dtype
    qpos = pos0 + jnp.arange(L, dtype=jnp.int32)
    h = _rms(x, norm_g)
    (hq, hf, hi, hgate,
     dq, dk, dv, iq, ik, iw, dgate,
     nq, nck, ncv, nsk, nsv, nwk, nwv, nbg, ngate,
     ma, mb, mc) = _split(h @ w_in, IN_SPLITS)

    lbh = lb.reshape(HG_HEADS, HG_K).astype(jnp.float32)
    fr = hf.reshape(B, L, HG_HEADS, HG_K).astype(jnp.float32)
    f = lbh + (1.0 - lbh) * jax.nn.sigmoid(fr)
    logf = jnp.log(f)
    hk = (1.0 - lbh) * jax.nn.sigmoid(-fr)
    hqa = jax.nn.silu(hq).reshape(B, L, HG_HEADS, HG_K)
    hv = hi.reshape(B, L, HG_HEADS, HG_V)
    if s0 is None:
        s0 = jnp.zeros((B, HG_HEADS, HG_K, HG_V), jnp.float32)
    ho, s_new = _hgrn2(hqa, hk, hv, logf, s0)
    ho = _rms(ho.astype(dt), hg_onorm_g.reshape(HG_HEADS, HG_V)).reshape(B, L, BRANCH_W)
    y_a = ho * jax.nn.silu(hgate)

    dq_h = _rope(_rms(dq.reshape(B, L, DSA_HEADS, HEAD_DIM), dsa_qk_g[0]), qpos)
    dk_h = _rope(_rms(dk.reshape(B, L, DSA_KV_HEADS, HEAD_DIM), dsa_qk_g[1]), qpos)
    iq_h = _rope(iq.reshape(B, L, IDX_HEADS, IDX_DIM), qpos)
    ik_h = _rope(_rms(ik, idx_k_g)[:, :, None, :], qpos)[:, :, 0]
    nq_h = _rope(_rms(nq.reshape(B, L, NSA_HEADS, HEAD_DIM), nsa_qk_g[0]), qpos)
    nck_h = _rope(_rms(nck.reshape(B, L, NSA_KV_HEADS, HEAD_DIM), nsa_qk_g[1]), qpos)
    nsk_h = _rope(_rms(nsk.reshape(B, L, NSA_KV_HEADS, HEAD_DIM), nsa_qk_g[2]), qpos)
    nwk_h = _rope(_rms(nwk.reshape(B, L, NSA_KV_HEADS, HEAD_DIM), nsa_qk_g[3]), qpos)
    flat = lambda a: a.reshape(B, L, -1)
    rows = jnp.concatenate([flat(dk_h), dv, ik_h, flat(nck_h), ncv, flat(nsk_h), nsv], axis=-1)
    full = rows if past_rows is None else jnp.concatenate([past_rows.astype(dt), rows], axis=1)
    lk = full.shape[1]
    fdk, fdv, fik, fck, fcv, fsk, fsv = _split(full, CACHE_SPLITS)
    kvh = lambda a, n: a.reshape(B, lk, n, HEAD_DIM)

    yb = _dsa_attend(dq_h, iq_h, iw * IDX_HEADS ** -0.5, qpos,
                     kvh(fdk, DSA_KV_HEADS), kvh(fdv, DSA_KV_HEADS), fik)
    y_b = yb.astype(dt) * jax.nn.silu(dgate)

    ck = _compress(kvh(fck, NSA_KV_HEADS), cmp_pe[0], cmp_w1[0], cmp_b1[0], cmp_w2[0])
    cv = _compress(kvh(fcv, NSA_KV_HEADS), cmp_pe[1], cmp_w1[1], cmp_b1[1], cmp_w2[1])
    o_cs = _nsa_cmp_sel(nq_h, qpos, ck, cv, kvh(fsk, NSA_KV_HEADS), kvh(fsv, NSA_KV_HEADS))
    wrows = jnp.concatenate([flat(nwk_h), nwv], axis=-1)
    wfull = wrows if win_rows is None else jnp.concatenate([win_rows.astype(dt), wrows], axis=1)
    lw = wfull.shape[1]
    wk, wv = jnp.split(wfull, 2, axis=-1)
    o_w = _window_attend(nq_h, qpos, wk.reshape(B, lw, NSA_KV_HEADS, HEAD_DIM),
                         wv.reshape(B, lw, NSA_KV_HEADS, HEAD_DIM), pos0 + L - lw)
    gb = jax.nn.sigmoid(nbg.reshape(B, L, NSA_HEADS, 3).astype(jnp.float32))
    o_c = gb[..., 0:1] * o_cs[:, :, 0] + gb[..., 1:2] * o_cs[:, :, 1] + gb[..., 2:3] * o_w
    y_c = o_c.reshape(B, L, BRANCH_W).astype(dt) * jax.nn.silu(ngate)

    ys = jnp.stack([y_a, y_b, y_c], axis=2)
    gates = jax.nn.sigmoid(jnp.stack([ma, mb, mc], axis=2))
    m = jnp.sum(gates * jnp.einsum('blnc,ncd->blnd', ys, w_branch), axis=2)
    y = x + m @ w_out
    return y, rows, wfull[:, -min(WINDOW, lw):], s_new.astype(dt)


def setup_inputs(seed: int = 0) -> dict:
    key = jax.random.key(seed)
    ks = jax.random.split(key, 20)
    nrm = lambda k, shape, scale: scale * jax.random.normal(k, shape, jnp.float32)
    n_pages = PAST_LEN // PAGE_SIZE
    n_used = DEC_BATCH * n_pages
    n_phys = n_used + (n_used + 3) // 4
    page_table = jax.random.permutation(ks[2], n_phys)[:n_used].reshape(DEC_BATCH, n_pages).astype(jnp.int32)
    wc = min(WINDOW, PAST_LEN)
    return {
        'x_prompt': nrm(ks[0], (BATCH, SEQ, D_MODEL), 1.0),
        'x_sample': nrm(ks[1], (DEC_BATCH, DEC_SEQ, D_MODEL), 1.0),
        'cache_kv': nrm(ks[3], (DEPTH, n_phys, PAGE_SIZE, CACHE_W), 1.0),
        'page_table': page_table,
        'state_win': nrm(ks[4], (DEPTH, DEC_BATCH, wc, WIN_W), 1.0),
        'state_hgrn': nrm(ks[5], (DEPTH, DEC_BATCH, HG_HEADS, HG_K, HG_V), 0.3),
        'norm_g': 1.0 + nrm(ks[6], (DEPTH, D_MODEL), 0.02),
        'w_in': nrm(ks[7], (DEPTH, D_MODEL, IN_COLS), D_MODEL ** -0.5),
        'hgrn_lb_logits': nrm(ks[8], (DEPTH, HG_HEADS * HG_K), 0.5),
        'hgrn_onorm_g': 1.0 + nrm(ks[9], (DEPTH, HG_HEADS * HG_V), 0.02),
        'dsa_qk_norm_g': 1.0 + nrm(ks[10], (DEPTH, 2, HEAD_DIM), 0.02),
        'dsa_idx_k_norm_g': 1.0 + nrm(ks[11], (DEPTH, IDX_DIM), 0.02),
        'nsa_qk_norm_g': 1.0 + nrm(ks[12], (DEPTH, 4, HEAD_DIM), 0.02),
        'nsa_cmp_pe': nrm(ks[13], (DEPTH, 2, CMP_BLOCK, HEAD_DIM), 0.1),
        'nsa_cmp_w1': nrm(ks[14], (DEPTH, 2, CMP_BLOCK * HEAD_DIM, CMP_HIDDEN), (CMP_BLOCK * HEAD_DIM) ** -0.5),
        'nsa_cmp_b1': nrm(ks[15], (DEPTH, 2, CMP_HIDDEN), 0.01),
        'nsa_cmp_w2': nrm(ks[16], (DEPTH, 2, CMP_HIDDEN, HEAD_DIM), CMP_HIDDEN ** -0.5),
        'w_branch': nrm(ks[17], (DEPTH, 3, BRANCH_W, D_MODEL), BRANCH_W ** -0.5),
        'w_out': nrm(ks[18], (DEPTH, D_MODEL, D_MODEL), D_MODEL ** -0.5),
    }


def reference(x_prompt, x_sample, cache_kv, page_table, state_win, state_hgrn, norm_g, w_in,
              hgrn_lb_logits, hgrn_onorm_g, dsa_qk_norm_g, dsa_idx_k_norm_g, nsa_qk_norm_g,
              nsa_cmp_pe, nsa_cmp_w1, nsa_cmp_b1, nsa_cmp_w2, w_branch, w_out):
    lb_soft = jax.nn.softmax(hgrn_lb_logits.astype(jnp.float32), axis=0)
    lbs = jnp.cumsum(lb_soft, axis=0) - lb_soft[0]
    n_seq, n_pages = page_table.shape
    xp, xs = x_prompt, x_sample
    p_rows, p_win, p_hg, s_rows, s_win, s_hg = [], [], [], [], [], []
    for l in range(DEPTH):
        lw = (norm_g[l], w_in[l], hgrn_onorm_g[l], dsa_qk_norm_g[l], dsa_idx_k_norm_g[l], nsa_qk_norm_g[l],
              nsa_cmp_pe[l], nsa_cmp_w1[l], nsa_cmp_b1[l], nsa_cmp_w2[l], w_branch[l], w_out[l])
        past = cache_kv[l][page_table].reshape(n_seq, n_pages * PAGE_SIZE, CACHE_W)
        xp, pr, pw, ph = _layer(xp, 0, None, None, None, lbs[l], *lw)
        xs, sr, sw, sh = _layer(xs, PAST_LEN, past, state_win[l], state_hgrn[l], lbs[l], *lw)
        p_rows.append(pr)
        p_win.append(pw)
        p_hg.append(ph)
        s_rows.append(sr)
        s_win.append(sw)
        s_hg.append(sh)
    return (xp, xs, jnp.stack(p_rows), jnp.stack(p_win), jnp.stack(p_hg),
            jnp.stack(s_rows), jnp.stack(s_win), jnp.stack(s_hg))
```

```python
import jax
import jax.numpy as jnp
from jax import lax
import numpy as np
from jax.experimental import pallas as pl

D_MODEL = 1024
PAGE_SIZE = 128
BRANCH_W = D_MODEL // 2
HEAD_DIM = 64
ROPE_THETA = 500000.0
EPS = 1e-6
NEG = -1e30
FORCE = 1e9
Q_BLOCK = 128
HG_HEADS = 4
HG_K = BRANCH_W // HG_HEADS
HG_V = BRANCH_W // HG_HEADS
HG_CHUNK = 64
DSA_HEADS = BRANCH_W // HEAD_DIM
DSA_KV_HEADS = 2
IDX_HEADS = 8
IDX_DIM = 64
DSA_TOPK = 256
NSA_HEADS = BRANCH_W // HEAD_DIM
NSA_KV_HEADS = 2
CMP_BLOCK = 32
CMP_STRIDE = 16
CMP_HIDDEN = HEAD_DIM
SEL_BLOCK = 64
SEL_TOPN = 16
WINDOW = 512
DSA_KV_W = DSA_KV_HEADS * HEAD_DIM
NSA_KV_W = NSA_KV_HEADS * HEAD_DIM
IN_SPLITS = (
    HG_HEADS * HG_K, HG_HEADS * HG_K, HG_HEADS * HG_V, BRANCH_W,
    DSA_HEADS * HEAD_DIM, DSA_KV_W, DSA_KV_W, IDX_HEADS * IDX_DIM, IDX_DIM, IDX_HEADS, BRANCH_W,
    NSA_HEADS * HEAD_DIM, NSA_KV_W, NSA_KV_W, NSA_KV_W, NSA_KV_W, NSA_KV_W, NSA_KV_W, NSA_HEADS * 3, BRANCH_W,
    D_MODEL, D_MODEL, D_MODEL,
)
IN_COLS = sum(IN_SPLITS)
CACHE_SPLITS = (DSA_KV_W, DSA_KV_W, IDX_DIM, NSA_KV_W, NSA_KV_W, NSA_KV_W, NSA_KV_W)
CACHE_W = sum(CACHE_SPLITS)
WIN_W = 2 * NSA_KV_W


def _mm_kernel(x_ref, w_ref, o_ref):
    o_ref[...] = jnp.dot(x_ref[...], w_ref[...], preferred_element_type=jnp.float32)


def _mm(x, w):
    lead = x.shape[:-1]
    k = x.shape[-1]
    n = w.shape[-1]
    x2 = x.reshape(-1, k)
    m = x2.shape[0]
    tm = min(512, m)
    tn = 512
    n_pad = -(-n // tn) * tn
    wp = jnp.pad(w, ((0, 0), (0, n_pad - n)))
    out = pl.pallas_call(
        _mm_kernel,
        grid=(m // tm, n_pad // tn),
        in_specs=[pl.BlockSpec((tm, k), lambda i, j: (i, 0)),
                  pl.BlockSpec((k, tn), lambda i, j: (0, j))],
        out_specs=pl.BlockSpec((tm, tn), lambda i, j: (i, j)),
        out_shape=jax.ShapeDtypeStruct((m, n_pad), jnp.float32),
    )(x2, wp)
    return out[:, :n].reshape(*lead, n)


def _split(z, widths):
    cuts = [int(c) for c in np.cumsum(widths)[:-1]]
    return jnp.split(z, cuts, axis=-1)


def _rms(x, g):
    x32 = x.astype(jnp.float32)
    y = x32 * lax.rsqrt(jnp.mean(x32 * x32, axis=-1, keepdims=True) + EPS)
    return (y * g.astype(jnp.float32)).astype(x.dtype)


def _rope(x, pos):
    half = (x.shape[-1] // 4) // 2
    inv = ROPE_THETA ** (-jnp.arange(half, dtype=jnp.float32) / half)
    ang = pos.astype(jnp.float32)[:, None] * inv[None, :]
    cos = jnp.cos(ang)[:, None, :].astype(x.dtype)
    sin = jnp.sin(ang)[:, None, :].astype(x.dtype)
    x1, x2, xp = x[..., :half], x[..., half:2 * half], x[..., 2 * half:]
    return jnp.concatenate([x1 * cos - x2 * sin, x1 * sin + x2 * cos, xp], axis=-1)


def _map_query_blocks(fn, qpos, *arrs):
    lq = qpos.shape[0]
    qb = min(Q_BLOCK, lq)
    pad = (-lq) % qb
    nb = (lq + pad) // qb
    qpos_b = jnp.pad(qpos, (0, pad), mode='edge').reshape(nb, qb)
    blocks = []
    for a in arrs:
        a = jnp.pad(a, [(0, 0), (0, pad)] + [(0, 0)] * (a.ndim - 2), mode='edge')
        blocks.append(jnp.moveaxis(a.reshape(a.shape[0], nb, qb, *a.shape[2:]), 1, 0))
    out = lax.map(lambda args: fn(*args), (qpos_b, *blocks))
    out = jnp.moveaxis(out, 0, 1)
    return out.reshape(out.shape[0], nb * qb, *out.shape[3:])[:, :lq]


def _hgrn2(q, k, v, logf, s0):
    B, L, H, _ = q.shape
    c = min(HG_CHUNK, L)
    pad = (-L) % c
    n = (L + pad) // c

    def chunks(a):
        a = jnp.pad(a.astype(jnp.float32), ((0, 0), (0, pad), (0, 0), (0, 0)))
        return a.reshape(B, n, c, H, a.shape[-1]).transpose(1, 0, 3, 2, 4)

    tril = jnp.tril(jnp.ones((c, c), dtype=bool))

    def step(S, inp):
        qc, kc, vc, gc = inp
        b = jnp.cumsum(gc, axis=2)
        diff = b[:, :, :, None, :] - b[:, :, None, :, :]
        decay = jnp.exp(jnp.where(tril[:, :, None], diff, -jnp.inf))
        A = jnp.einsum('bhtk,bhsk,bhtsk->bhts', qc, kc, decay)
        o = jnp.einsum('bhts,bhsv->bhtv', A, vc) + jnp.einsum('bhtk,bhkv->bhtv', qc * jnp.exp(b), S)
        bl = b[:, :, -1:, :]
        S = jnp.exp(bl[:, :, 0, :, None]) * S + jnp.einsum('bhsk,bhsv->bhkv', kc * jnp.exp(bl - b), vc)
        return S, o

    S, o = lax.scan(step, s0.astype(jnp.float32), (chunks(q), chunks(k), chunks(v), chunks(logf)))
    o = o.transpose(1, 0, 3, 2, 4).reshape(B, n * c, H, v.shape[-1])[:, :L]
    return o, S


def _dsa_attend(q, iq, iw, qpos, k, v, ik):
    lk = k.shape[1]
    n_top = min(DSA_TOPK, lk // 4)
    kpos = jnp.arange(lk, dtype=jnp.int32)
    g = DSA_HEADS // DSA_KV_HEADS
    take = jax.vmap(lambda a, i: a[i])

    def block(qp, qb, iqb, iwb):
        bq, t = qb.shape[:2]
        dots = jnp.einsum('bthd,bsd->bths', iqb, ik, preferred_element_type=jnp.float32) * IDX_DIM ** -0.5
        score = jnp.einsum('bth,bths->bts', iwb.astype(jnp.float32), jax.nn.relu(dots))
        score = jnp.where(kpos[None, None, :] <= qp[None, :, None], score, NEG)
        sel = lax.top_k(score, n_top)[1]
        kg, vg = take(k, sel), take(v, sel)
        qg = qb.reshape(bq, t, DSA_KV_HEADS, g, HEAD_DIM)
        s = jnp.einsum('btkgd,btjkd->btkgj', qg, kg, preferred_element_type=jnp.float32) * HEAD_DIM ** -0.5
        s = jnp.where((sel <= qp[None, :, None])[:, :, None, None, :], s, NEG)
        o = jnp.einsum('btkgj,btjkd->btkgd', jax.nn.softmax(s, axis=-1), vg.astype(jnp.float32))
        return o.reshape(bq, t, DSA_HEADS * HEAD_DIM)

    return _map_query_blocks(block, qpos, q, iq, iw)


def _compress(kv, pe, w1, b1, w2):
    B, lk, h, d = kv.shape
    n_cmp = (lk - CMP_BLOCK) // CMP_STRIDE + 1
    idx = jnp.arange(n_cmp)[:, None] * CMP_STRIDE + jnp.arange(CMP_BLOCK)[None, :]
    blk = kv[:, idx] + pe[:, None, :]
    blk = jnp.swapaxes(blk, 2, 3).reshape(B, n_cmp, h, CMP_BLOCK * d)
    return jax.nn.silu(blk @ w1 + b1) @ w2


def _nsa_cmp_sel(q, qpos, ck, cv, sk, sv):
    B, lk = sk.shape[:2]
    n_cmp = ck.shape[1]
    n_sel = -(-lk // SEL_BLOCK)
    n_top = min(SEL_TOPN, n_sel)
    g = NSA_HEADS // NSA_KV_HEADS
    scale = HEAD_DIM ** -0.5
    c_start = jnp.arange(n_cmp, dtype=jnp.int32) * CMP_STRIDE
    c_end = c_start + CMP_BLOCK - 1
    s_start = jnp.arange(n_sel, dtype=jnp.int32) * SEL_BLOCK
    cover = jnp.clip(jnp.minimum(c_start[:, None] + CMP_BLOCK, s_start[None, :] + SEL_BLOCK)
                     - jnp.maximum(c_start[:, None], s_start[None, :]), 0, None).astype(jnp.float32) / CMP_BLOCK
    padk = ((0, 0), (0, n_sel * SEL_BLOCK - lk), (0, 0), (0, 0))
    skb = jnp.pad(sk, padk).reshape(B, n_sel, SEL_BLOCK, NSA_KV_HEADS, HEAD_DIM).transpose(0, 3, 1, 2, 4)
    svb = jnp.pad(sv, padk).reshape(B, n_sel, SEL_BLOCK, NSA_KV_HEADS, HEAD_DIM).transpose(0, 3, 1, 2, 4)
    heads_idx = jnp.arange(NSA_KV_HEADS)[None, :, None]
    take_blocks = jax.vmap(lambda a, i: a[heads_idx, i])

    def block(qp, qb):
        bq, t = qb.shape[:2]
        qg = qb.reshape(bq, t, NSA_KV_HEADS, g, HEAD_DIM)
        cvalid = (c_end[None, :] <= qp[:, None])[None, :, None, None, :]
        s = jnp.einsum('btkgd,bckd->btkgc', qg, ck, preferred_element_type=jnp.float32) * scale
        p = jnp.where(cvalid, jax.nn.softmax(jnp.where(cvalid, s, NEG), axis=-1), 0.0)
        o_cmp = jnp.einsum('btkgc,bckd->btkgd', p, cv.astype(jnp.float32))
        imp = jnp.einsum('btkgc,cj->btkj', p, cover)
        tb = (qp // SEL_BLOCK)[:, None]
        blk = jnp.arange(n_sel, dtype=jnp.int32)[None, :]
        forced = ((blk == 0) | (blk == tb) | (blk == tb - 1))[None, :, None, :]
        causal = (s_start[None, :] <= qp[:, None])[None, :, None, :]
        imp = jnp.where(causal, jnp.where(forced, FORCE, imp), NEG)
        sel = lax.top_k(imp, n_top)[1]
        kg = take_blocks(skb, sel).reshape(bq, t, NSA_KV_HEADS, n_top * SEL_BLOCK, HEAD_DIM)
        vg = take_blocks(svb, sel).reshape(bq, t, NSA_KV_HEADS, n_top * SEL_BLOCK, HEAD_DIM)
        kpos = (sel[..., None] * SEL_BLOCK + jnp.arange(SEL_BLOCK, dtype=jnp.int32)).reshape(
            bq, t, NSA_KV_HEADS, n_top * SEL_BLOCK)
        s2 = jnp.einsum('btkgd,btkjd->btkgj', qg, kg, preferred_element_type=jnp.float32) * scale
        s2 = jnp.where((kpos <= qp[None, :, None, None])[:, :, :, None, :], s2, NEG)
        o_sel = jnp.einsum('btkgj,btkjd->btkgd', jax.nn.softmax(s2, axis=-1), vg.astype(jnp.float32))
        return jnp.stack([o_cmp, o_sel], axis=2).reshape(bq, t, 2, NSA_HEADS, HEAD_DIM)

    return _map_query_blocks(block, qpos, q)


def _window_attend(q, qpos, k, v, kpos0):
    qb_len = min(Q_BLOCK, qpos.shape[0])
    band = WINDOW + qb_len
    padw = ((0, 0), (WINDOW, qb_len), (0, 0), (0, 0))
    kp, vp = jnp.pad(k, padw), jnp.pad(v, padw)
    g = NSA_HEADS // NSA_KV_HEADS

    def block(qp, qb):
        bq, t = qb.shape[:2]
        start = qp[0] - kpos0
        kb = lax.dynamic_slice_in_dim(kp, start, band, axis=1)
        vb = lax.dynamic_slice_in_dim(vp, start, band, axis=1)
        pos = qp[0] - WINDOW + jnp.arange(band, dtype=jnp.int32)
        valid = (pos[None, :] >= kpos0) & (pos[None, :] <= qp[:, None]) & (qp[:, None] - pos[None, :] < WINDOW)
        qg = qb.reshape(bq, t, NSA_KV_HEADS, g, HEAD_DIM)
        s = jnp.einsum('btkgd,bskd->btkgs', qg, kb, preferred_element_type=jnp.float32) * HEAD_DIM ** -0.5
        s = jnp.where(valid[None, :, None, None, :], s, NEG)
        o = jnp.einsum('btkgs,bskd->btkgd', jax.nn.softmax(s, axis=-1), vb.astype(jnp.float32))
        return o.reshape(bq, t, NSA_HEADS, HEAD_DIM)

    return _map_query_blocks(block, qpos, q)


def _layer(x, pos0, past_rows, win_rows, s0, lb, norm_g, w_in, hg_onorm_g, dsa_qk_g, idx_k_g,
           nsa_qk_g, cmp_pe, cmp_w1, cmp_b1, cmp_w2, w_branch, w_out):
    B, L, _ = x.shape
    dt = x.dtype
    qpos = pos0 + jnp.arange(L, dtype=jnp.int32)
    h = _rms(x, norm_g)
    (hq, hf, hi, hgate,
     dq, dk, dv, iq, ik, iw, dgate,
     nq, nck, ncv, nsk, nsv, nwk, nwv, nbg, ngate,
     ma, mb, mc) = _split(_mm(h, w_in), IN_SPLITS)

    lbh = lb.reshape(HG_HEADS, HG_K).astype(jnp.float32)
    fr = hf.reshape(B, L, HG_HEADS, HG_K).astype(jnp.float32)
    f = lbh + (1.0 - lbh) * jax.nn.sigmoid(fr)
    logf = jnp.log(f)
    hk = (1.0 - lbh) * jax.nn.sigmoid(-fr)
    hqa = jax.nn.silu(hq).reshape(B, L, HG_HEADS, HG_K)
    hv = hi.reshape(B, L, HG_HEADS, HG_V)
    if s0 is None:
        s0 = jnp.zeros((B, HG_HEADS, HG_K, HG_V), jnp.float32)
    ho, s_new = _hgrn2(hqa, hk, hv, logf, s0)
    ho = _rms(ho.astype(dt), hg_onorm_g.reshape(HG_HEADS, HG_V)).reshape(B, L, BRANCH_W)
    y_a = ho * jax.nn.silu(hgate)

    dq_h = _rope(_rms(dq.reshape(B, L, DSA_HEADS, HEAD_DIM), dsa_qk_g[0]), qpos)
    dk_h = _rope(_rms(dk.reshape(B, L, DSA_KV_HEADS, HEAD_DIM), dsa_qk_g[1]), qpos)
    iq_h = _rope(iq.reshape(B, L, IDX_HEADS, IDX_DIM), qpos)
    ik_h = _rope(_rms(ik, idx_k_g)[:, :, None, :], qpos)[:, :, 0]
    nq_h = _rope(_rms(nq.reshape(B, L, NSA_HEADS, HEAD_DIM), nsa_qk_g[0]), qpos)
    nck_h = _rope(_rms(nck.reshape(B, L, NSA_KV_HEADS, HEAD_DIM), nsa_qk_g[1]), qpos)
    nsk_h = _rope(_rms(nsk.reshape(B, L, NSA_KV_HEADS, HEAD_DIM), nsa_qk_g[2]), qpos)
    nwk_h = _rope(_rms(nwk.reshape(B, L, NSA_KV_HEADS, HEAD_DIM), nsa_qk_g[3]), qpos)
    flat = lambda a: a.reshape(B, L, -1)
    rows = jnp.concatenate([flat(dk_h), dv, ik_h, flat(nck_h), ncv, flat(nsk_h), nsv], axis=-1)
    full = rows if past_rows is None else jnp.concatenate([past_rows.astype(dt), rows], axis=1)
    lk = full.shape[1]
    fdk, fdv, fik, fck, fcv, fsk, fsv = _split(full, CACHE_SPLITS)
    kvh = lambda a, n: a.reshape(B, lk, n, HEAD_DIM)

    yb = _dsa_attend(dq_h, iq_h, iw * IDX_HEADS ** -0.5, qpos,
                     kvh(fdk, DSA_KV_HEADS), kvh(fdv, DSA_KV_HEADS), fik)
    y_b = yb.astype(dt) * jax.nn.silu(dgate)

    ck = _compress(kvh(fck, NSA_KV_HEADS), cmp_pe[0], cmp_w1[0], cmp_b1[0], cmp_w2[0])
    cv = _compress(kvh(fcv, NSA_KV_HEADS), cmp_pe[1], cmp_w1[1], cmp_b1[1], cmp_w2[1])
    o_cs = _nsa_cmp_sel(nq_h, qpos, ck, cv, kvh(fsk, NSA_KV_HEADS), kvh(fsv, NSA_KV_HEADS))
    wrows = jnp.concatenate([flat(nwk_h), nwv], axis=-1)
    wfull = wrows if win_rows is None else jnp.concatenate([win_rows.astype(dt), wrows], axis=1)
    lw = wfull.shape[1]
    wk, wv = jnp.split(wfull, 2, axis=-1)
    o_w = _window_attend(nq_h, qpos, wk.reshape(B, lw, NSA_KV_HEADS, HEAD_DIM),
                         wv.reshape(B, lw, NSA_KV_HEADS, HEAD_DIM), pos0 + L - lw)
    gb = jax.nn.sigmoid(nbg.reshape(B, L, NSA_HEADS, 3).astype(jnp.float32))
    o_c = gb[..., 0:1] * o_cs[:, :, 0] + gb[..., 1:2] * o_cs[:, :, 1] + gb[..., 2:3] * o_w
    y_c = o_c.reshape(B, L, BRANCH_W).astype(dt) * jax.nn.silu(ngate)

    ys = jnp.stack([y_a, y_b, y_c], axis=2)
    gates = jax.nn.sigmoid(jnp.stack([ma, mb, mc], axis=2))
    m = jnp.sum(gates * jnp.stack([_mm(ys[:, :, i], w_branch[i]) for i in range(3)], axis=2), axis=2)
    y = x + _mm(m, w_out)
    return y, rows, wfull[:, -min(WINDOW, lw):], s_new.astype(dt)


def kernel(x_prompt, x_sample, cache_kv, page_table, state_win, state_hgrn, norm_g, w_in,
           hgrn_lb_logits, hgrn_onorm_g, dsa_qk_norm_g, dsa_idx_k_norm_g, nsa_qk_norm_g,
           nsa_cmp_pe, nsa_cmp_w1, nsa_cmp_b1, nsa_cmp_w2, w_branch, w_out):
    lb_soft = jax.nn.softmax(hgrn_lb_logits.astype(jnp.float32), axis=0)
    lbs = jnp.cumsum(lb_soft, axis=0) - lb_soft[0]
    n_seq, n_pages = page_table.shape
    past_len = n_pages * PAGE_SIZE
    depth = w_in.shape[0]
    xp, xs = x_prompt, x_sample
    p_rows, p_win, p_hg, s_rows, s_win, s_hg = [], [], [], [], [], []
    for l in range(depth):
        lw = (norm_g[l], w_in[l], hgrn_onorm_g[l], dsa_qk_norm_g[l], dsa_idx_k_norm_g[l], nsa_qk_norm_g[l],
              nsa_cmp_pe[l], nsa_cmp_w1[l], nsa_cmp_b1[l], nsa_cmp_w2[l], w_branch[l], w_out[l])
        past = cache_kv[l][page_table].reshape(n_seq, n_pages * PAGE_SIZE, CACHE_W)
        xp, pr, pw, ph = _layer(xp, 0, None, None, None, lbs[l], *lw)
        xs, sr, sw, sh = _layer(xs, past_len, past, state_win[l], state_hgrn[l], lbs[l], *lw)
        p_rows.append(pr)
        p_win.append(pw)
        p_hg.append(ph)
        s_rows.append(sr)
        s_win.append(sw)
        s_hg.append(sh)
    return (xp, xs, jnp.stack(p_rows), jnp.stack(p_win), jnp.stack(p_hg),
            jnp.stack(s_rows), jnp.stack(s_win), jnp.stack(s_hg))
```

```python
import functools

import jax
import jax.numpy as jnp
from jax import lax
import numpy as np
from jax.experimental import pallas as pl
from jax.experimental.pallas import tpu as pltpu

F32 = jnp.float32
BF16 = jnp.bfloat16
I32 = jnp.int32

D_MODEL = 1024
PAGE_SIZE = 128
BRANCH_W = D_MODEL // 2
HEAD_DIM = 64
ROPE_THETA = 500000.0
EPS = 1e-6
NEG = -1e30
FORCE = 1e9
HG_HEADS = 4
HG_K = BRANCH_W // HG_HEADS
HG_V = BRANCH_W // HG_HEADS
HG_CHUNK = 64
N_HEADS = BRANCH_W // HEAD_DIM
KV_HEADS = 2
GROUP = N_HEADS // KV_HEADS
IDX_HEADS = 8
IDX_DIM = 64
DSA_TOPK = 256
CMP_BLOCK = 32
CMP_STRIDE = 16
SEL_BLOCK = 64
SEL_TOPN = 16
WINDOW = 512
KV_W = KV_HEADS * HEAD_DIM
CACHE_W = 2 * KV_W + IDX_DIM + 4 * KV_W
WIN_W = 2 * KV_W
LANES = 128
QB = 128
INT_MIN = np.int32(-2 ** 31)
VMEM_LIMIT = 56 * 1024 * 1024

IN_SPLITS = (
    ("hq", 512), ("hf", 512), ("hi", 512), ("hgate", 512),
    ("dq", 512), ("dk", 128), ("dv", 128), ("iq", 512), ("ik", 64), ("iw", 8), ("dgate", 512),
    ("nq", 512), ("nck", 128), ("ncv", 128), ("nsk", 128), ("nsv", 128), ("nwk", 128), ("nwv", 128),
    ("nbg", 24), ("ngate", 512), ("ma", 1024), ("mb", 1024), ("mc", 1024),
)
Z_ORDER = ("ma", "mb", "mc", "hq", "hf", "hi", "hgate", "dq", "iq", "nq", "dgate", "ngate",
           "dk", "dv", "ik", "iw", "nck", "ncv", "nsk", "nsv", "nwk", "nwv", "nbg")
Z_COLS = 9216


def _z_layout():
    widths = dict(IN_SPLITS)
    src, off = {}, 0
    for name, w in IN_SPLITS:
        src[name] = off
        off += w
    dst, off = {}, 0
    for name in Z_ORDER:
        dst[name] = off
        off += -(-widths[name] // LANES) * LANES
    assert off <= Z_COLS
    idx = np.zeros((Z_COLS,), np.int32)
    valid = np.zeros((Z_COLS,), np.float32)
    for name in Z_ORDER:
        w = widths[name]
        idx[dst[name]:dst[name] + w] = src[name] + np.arange(w)
        valid[dst[name]:dst[name] + w] = 1.0
    return dst, idx, valid


Z_OFF, Z_SRC_IDX, Z_VALID = _z_layout()


def _cparams(*sem):
    return pltpu.CompilerParams(dimension_semantics=sem, vmem_limit_bytes=VMEM_LIMIT)


def _proj_kernel(x_ref, g_ref, w_ref, z_ref, h_scr):
    @pl.when(pl.program_id(1) == 0)
    def _():
        x = x_ref[...]
        ms = jnp.mean(x * x, axis=-1, keepdims=True)
        h_scr[...] = (x * lax.rsqrt(ms + EPS) * g_ref[...]).astype(BF16)

    z_ref[...] = jnp.dot(h_scr[...], w_ref[...], preferred_element_type=F32)


def _proj(x2, g, w16):
    t, d = x2.shape
    tm = min(t, 512)
    tn = 1536
    return pl.pallas_call(
        _proj_kernel,
        name="proj",
        grid=(t // tm, Z_COLS // tn),
        in_specs=[pl.BlockSpec((tm, d), lambda i, j: (i, 0)),
                  pl.BlockSpec((1, d), lambda i, j: (0, 0)),
                  pl.BlockSpec((d, tn), lambda i, j: (0, j))],
        out_specs=pl.BlockSpec((tm, tn), lambda i, j: (i, j)),
        out_shape=jax.ShapeDtypeStruct((t, Z_COLS), F32),
        scratch_shapes=[pltpu.VMEM((tm, d), BF16)],
        compiler_params=_cparams("parallel", "arbitrary"),
    )(x2, g.reshape(1, d), w16)


KV_KD, KV_VD, KV_KI, KV_KS, KV_VS, KV_KW, KV_VW = range(7)
G_DQ, G_DK, G_IK, G_NQ, G_NCK, G_NSK, G_NWK = range(7)


def _head_rms(x, gain, bd):
    x2 = x * x
    hi = x2.astype(BF16)
    lo = (x2 - hi.astype(F32)).astype(BF16)
    ms = jnp.dot(hi, bd, preferred_element_type=F32) + jnp.dot(lo, bd, preferred_element_type=F32)
    return x * lax.rsqrt(ms + EPS) * gain


def _rope128(x, cosf, sina, sinb):
    return x * cosf + pltpu.roll(x, LANES - 8, 1) * sina + pltpu.roll(x, 8, 1) * sinb


def _prep_kernel(dq_ref, iq_ref, nq_ref, s0_ref, s1_ref, s2_ref, cos_ref, sina_ref, sinb_ref, g_ref, bd_ref,
                 rows_ref, wrows_ref, cmp_ref, qd_ref, qi_ref, qn_ref, iw_ref, kv_ref):
    cosf, sina, sinb = cos_ref[...], sina_ref[...], sinb_ref[...]
    bd = bd_ref[...]
    lane = lax.broadcasted_iota(I32, cosf.shape, 1)
    lo_half = lane < HEAD_DIM
    rope = lambda v: _rope128(v, cosf, sina, sinb)
    gain = lambda r: g_ref[r:r + 1, :]

    def put_q(src_ref, out_ref, g_row, scale, target_of_head):
        for j in range(N_HEADS // 2):
            x = src_ref[:, j * LANES:(j + 1) * LANES]
            if g_row is not None:
                x = _head_rms(x, gain(g_row), bd)
            x = rope(x) * scale
            xr = pltpu.roll(x, HEAD_DIM, 1)
            for half in range(2):
                h = 2 * j + half
                tgt = target_of_head(h)
                v = x if tgt == half else xr
                keep = lo_half if tgt == 0 else jnp.logical_not(lo_half)
                out_ref[:, h * LANES:(h + 1) * LANES] = jnp.where(keep, v, 0.0).astype(BF16)

    put_q(dq_ref, qd_ref, G_DQ, HEAD_DIM ** -0.5, lambda h: h // GROUP)
    put_q(iq_ref, qi_ref, None, 1.0, lambda h: 0)
    put_q(nq_ref, qn_ref, G_NQ, HEAD_DIM ** -0.5, lambda h: h // GROUP)

    dk = rope(_head_rms(s0_ref[:, 0:128], gain(G_DK), bd))
    dv = s0_ref[:, 128:256]
    ik = rope(_head_rms(s0_ref[:, 256:384], gain(G_IK), bd))
    iw_ref[...] = s0_ref[:, 384:512] * (IDX_HEADS ** -0.5 * IDX_DIM ** -0.5)
    nck = rope(_head_rms(s1_ref[:, 0:128], gain(G_NCK), bd))
    ncv = s1_ref[:, 128:256]
    nsk = rope(_head_rms(s1_ref[:, 256:384], gain(G_NSK), bd))
    nsv = s1_ref[:, 384:512]
    nwk = rope(_head_rms(s2_ref[:, 0:128], gain(G_NWK), bd))
    nwv = s2_ref[:, 128:256]

    rows_ref[:, 0:128] = dk
    rows_ref[:, 128:256] = dv
    rows_ref[:, 256:320] = ik[:, 0:64]
    rows_ref[:, 320:448] = nck
    rows_ref[:, 448:576] = ncv
    rows_ref[:, 576:704] = nsk
    rows_ref[:, 704:832] = nsv
    wrows_ref[:, 0:128] = nwk
    wrows_ref[:, 128:256] = nwv
    cmp_ref[:, 0:128] = nck
    cmp_ref[:, 128:256] = ncv
    for blk, v in ((KV_KD, dk), (KV_VD, dv), (KV_KI, ik), (KV_KS, nsk), (KV_VS, nsv), (KV_KW, nwk), (KV_VW, nwv)):
        kv_ref[:, blk * LANES:(blk + 1) * LANES] = v.astype(BF16)


def _prep(z, tabs, gains, bd, tm):
    t = z.shape[0]
    nt = tabs[0].shape[0] // tm
    zb = lambda blk: pl.BlockSpec((tm, 512), lambda i: (i, blk))
    tab = pl.BlockSpec((tm, LANES), lambda i: (i % nt, 0))
    full = lambda a: pl.BlockSpec(a.shape, lambda i: (0,) * a.ndim)
    outs = [(CACHE_W, F32), (WIN_W, F32), (2 * KV_W, F32), (N_HEADS * LANES, BF16), (N_HEADS * LANES, BF16),
            (N_HEADS * LANES, BF16), (LANES, F32), (7 * LANES, BF16)]
    return pl.pallas_call(
        _prep_kernel,
        name="prep",
        grid=(t // tm,),
        in_specs=[zb(Z_OFF["dq"] // 512), zb(Z_OFF["iq"] // 512), zb(Z_OFF["nq"] // 512),
                  zb(Z_OFF["dk"] // 512), zb(Z_OFF["nck"] // 512), zb(Z_OFF["nwk"] // 512),
                  tab, tab, tab, full(gains), full(bd)],
        out_specs=[pl.BlockSpec((tm, w), lambda i: (i, 0)) for w, _ in outs],
        out_shape=[jax.ShapeDtypeStruct((t, w), dt) for w, dt in outs],
        compiler_params=_cparams("parallel"),
    )(z, z, z, z, z, z, *tabs, gains, bd)


def _rope_tables(pos):
    half = HEAD_DIM // 8
    inv = ROPE_THETA ** (-jnp.arange(half, dtype=F32) / half)
    ang = pos.astype(F32)[:, None] * inv[None, :]
    cos, sin = jnp.cos(ang), jnp.sin(ang)
    n = pos.shape[0]
    one = jnp.ones((n, HEAD_DIM - 2 * half), F32)
    zero = jnp.zeros((n, HEAD_DIM - 2 * half), F32)
    z8 = jnp.zeros((n, half), F32)
    cosf = jnp.concatenate([cos, cos, one], axis=1)
    sina = jnp.concatenate([-sin, z8, zero], axis=1)
    sinb = jnp.concatenate([z8, sin, zero], axis=1)
    return tuple(jnp.tile(a, (1, 2)) for a in (cosf, sina, sinb))


def _perm_w_in(w_in):
    return (jnp.take(w_in, jnp.asarray(Z_SRC_IDX), axis=1) * jnp.asarray(Z_VALID)[None, :]).astype(BF16)


def _prep_consts(dsa_qk_g, idx_k_g, nsa_qk_g):
    two = lambda g: jnp.tile(g.astype(F32), 2)
    ikg = jnp.concatenate([idx_k_g.astype(F32), jnp.ones((HEAD_DIM,), F32)])
    rows = [two(dsa_qk_g[0]), two(dsa_qk_g[1]), ikg, two(nsa_qk_g[0]), two(nsa_qk_g[1]), two(nsa_qk_g[2]),
            two(nsa_qk_g[3]), jnp.ones((LANES,), F32)]
    bd = np.kron(np.eye(2, dtype=np.float32), np.full((HEAD_DIM, HEAD_DIM), 1.0 / HEAD_DIM, np.float32))
    return jnp.stack(rows), jnp.asarray(bd, BF16)


def _split3(x):
    a = x.astype(BF16)
    r = x - a.astype(F32)
    b = r.astype(BF16)
    c = (r - b.astype(F32)).astype(BF16)
    return a, b, c


def _row_to_col(row, eye):
    return jnp.sum(jnp.where(eye, row, 0.0), axis=1, keepdims=True)


def _hgrn_gates(fr, lb):
    sig = jax.nn.sigmoid(fr)
    f = lb + (1.0 - lb) * sig
    return jnp.log(f), (1.0 - lb) * jax.nn.sigmoid(-fr)


def _hgrn_out(o, gon, gate):
    ms = jnp.mean(o * o, axis=-1, keepdims=True)
    return o * lax.rsqrt(ms + EPS) * gon * (gate * jax.nn.sigmoid(gate))


def _hgrn_prompt_kernel(hq_ref, hf_ref, hi_ref, hg_ref, lb_ref, gon_ref, ya_ref, s_ref, b_scr, k_scr):
    c = HG_CHUNK
    n_chunks = hq_ref.shape[0] // c
    lb, gon = lb_ref[...], gon_ref[...]
    ti = lax.broadcasted_iota(I32, (c, HG_K), 0)
    tril = (lax.broadcasted_iota(I32, (c, c), 0) >= lax.broadcasted_iota(I32, (c, c), 1)).astype(BF16)
    lane_c = lax.broadcasted_iota(I32, (c, c), 1)
    eye = lax.broadcasted_iota(I32, (HG_K, HG_K), 0) == lax.broadcasted_iota(I32, (HG_K, HG_K), 1)
    s_ref[0, 0] = jnp.zeros((HG_K, HG_V), F32)

    def chunk(ci, carry):
        rows = pl.ds(pl.multiple_of(ci * c, c), c)
        g, kk = _hgrn_gates(hf_ref[rows, :], lb)
        hq = hq_ref[rows, :]
        qq = hq * jax.nn.sigmoid(hq)
        v = hi_ref[rows, :]
        b = sum(jnp.dot(tril, part, preferred_element_type=F32) for part in _split3(g))
        b_scr[...] = b
        k_scr[...] = kk

        def col(s, a):
            diff = b - b_scr[pl.ds(s, 1), :]
            e = jnp.exp(jnp.where(ti >= s, diff, -jnp.inf))
            colv = jnp.sum(qq * e * k_scr[pl.ds(s, 1), :], axis=1, keepdims=True)
            return jnp.where(lane_c == s, colv, a)

        a = lax.fori_loop(0, c, col, jnp.zeros((c, c), F32))
        s_old = s_ref[0, 0]
        o = (jnp.dot(a.astype(BF16), v.astype(BF16), preferred_element_type=F32)
             + jnp.dot((qq * jnp.exp(b)).astype(BF16), s_old.astype(BF16), preferred_element_type=F32))
        bl = b[c - 1:c, :]
        kdec = (kk * jnp.exp(bl - b)).astype(BF16)
        s_ref[0, 0] = (_row_to_col(jnp.exp(bl), eye) * s_old
                       + lax.dot_general(kdec, v.astype(BF16), (((0,), (0,)), ((), ())),
                                         preferred_element_type=F32))
        ya_ref[rows, :] = _hgrn_out(o, gon, hg_ref[rows, :])
        return carry

    lax.fori_loop(0, n_chunks, chunk, 0)


def _hgrn_prompt(z, lb, gon, batch, seq):
    zb = lambda name: pl.BlockSpec((seq, HG_K), lambda b, h, o=Z_OFF[name] // HG_K: (b, o + h))
    vec = pl.BlockSpec((1, HG_K), lambda b, h: (0, h))
    return pl.pallas_call(
        _hgrn_prompt_kernel,
        name="hgrn_prompt",
        grid=(batch, HG_HEADS),
        in_specs=[zb("hq"), zb("hf"), zb("hi"), zb("hgate"), vec, vec],
        out_specs=[pl.BlockSpec((seq, HG_V), lambda b, h: (b, h)),
                   pl.BlockSpec((1, 1, HG_K, HG_V), lambda b, h: (b, h, 0, 0))],
        out_shape=[jax.ShapeDtypeStruct((batch * seq, BRANCH_W), F32),
                   jax.ShapeDtypeStruct((batch, HG_HEADS, HG_K, HG_V), F32)],
        scratch_shapes=[pltpu.VMEM((HG_CHUNK, HG_K), F32), pltpu.VMEM((HG_CHUNK, HG_K), F32)],
        compiler_params=_cparams("parallel", "parallel"),
    )(z, z, z, z, lb.reshape(1, BRANCH_W), gon.reshape(1, BRANCH_W))


def _hgrn_sample_kernel(hq_ref, hf_ref, hi_ref, hg_ref, lb_ref, gon_ref, s0_ref, ya_ref, s_ref):
    n = hq_ref.shape[0]
    lb, gon = lb_ref[...], gon_ref[...]
    eye = lax.broadcasted_iota(I32, (HG_K, HG_K), 0) == lax.broadcasted_iota(I32, (HG_K, HG_K), 1)

    def seq(b, carry):
        row = pl.ds(b, 1)
        g, kk = _hgrn_gates(hf_ref[row, :], lb)
        hq = hq_ref[row, :]
        qq = hq * jax.nn.sigmoid(hq)
        s_new = (_row_to_col(jnp.exp(g), eye) * s0_ref[b, 0] + _row_to_col(kk, eye) * hi_ref[row, :])
        s_ref[b, 0] = s_new
        o = jnp.sum(_row_to_col(qq, eye) * s_new, axis=0, keepdims=True)
        ya_ref[row, :] = _hgrn_out(o, gon, hg_ref[row, :])
        return carry

    lax.fori_loop(0, n, seq, 0)


def _hgrn_sample(z, lb, gon, s0):
    n = z.shape[0]
    zb = lambda name: pl.BlockSpec((n, HG_K), lambda h, o=Z_OFF[name] // HG_K: (0, o + h))
    vec = pl.BlockSpec((1, HG_K), lambda h: (0, h))
    st = pl.BlockSpec((n, 1, HG_K, HG_V), lambda h: (0, h, 0, 0))
    return pl.pallas_call(
        _hgrn_sample_kernel,
        name="hgrn_sample",
        grid=(HG_HEADS,),
        in_specs=[zb("hq"), zb("hf"), zb("hi"), zb("hgate"), vec, vec, st],
        out_specs=[pl.BlockSpec((n, HG_V), lambda h: (0, h)), st],
        out_shape=[jax.ShapeDtypeStruct((n, BRANCH_W), F32),
                   jax.ShapeDtypeStruct((n, HG_HEADS, HG_K, HG_V), F32)],
        compiler_params=_cparams("parallel"),
    )(z, z, z, z, lb.reshape(1, BRANCH_W), gon.reshape(1, BRANCH_W), s0)


def _compress_consts(pe, w1, b1, w2):
    eye2 = jnp.eye(KV_HEADS, dtype=F32)
    w1r = w1.reshape(2, CMP_BLOCK, HEAD_DIM, HEAD_DIM)
    kron = lambda a: jnp.einsum('ij,ktdh->ktidjh', eye2, a).reshape(2, CMP_STRIDE, KV_W, KV_W)
    w1ab = jnp.concatenate([kron(w1r[:, :CMP_STRIDE]), kron(w1r[:, CMP_STRIDE:])], axis=-1).astype(BF16)
    pe8 = jnp.zeros((2, 8, CMP_BLOCK * HEAD_DIM), F32).at[:, 0].set(pe.reshape(2, -1)).astype(BF16)
    w1d = jnp.tile(w1, (1, 1, KV_HEADS)).astype(BF16)
    b1d = jnp.tile(b1, (1, KV_HEADS)).reshape(2, 1, KV_W).astype(F32)
    w2bd = jnp.einsum('ij,kdh->kidjh', eye2, w2).reshape(2, KV_W, KV_W).astype(BF16)
    return w1ab, pe8, w1d, b1d, w2bd


def _compress_body(load_chunk_rows, n_ch, w1ab_ref, pe_ref, w1d_ref, b1_ref, w2_ref, pb_scr, j):
    acc = jnp.zeros((n_ch, 2 * KV_W), F32)
    for t in range(CMP_STRIDE):
        acc = acc + load_chunk_rows(j, t, w1ab_ref)
    c0 = b1_ref[j] + jnp.dot(pe_ref[j], w1d_ref[j], preferred_element_type=F32)[0:1, :]
    pb_scr[0:n_ch, :] = acc[:, KV_W:]
    pb_scr[n_ch:n_ch + 8, :] = jnp.zeros((8, KV_W), F32)
    hid = acc[:, :KV_W] + pb_scr[pl.ds(1, n_ch), :] + c0
    hid = hid * jax.nn.sigmoid(hid)
    return jnp.dot(hid.astype(BF16), w2_ref[j], preferred_element_type=F32)


def _compress_kernel(ksrc_ref, vsrc_ref, w1ab_ref, pe_ref, w1d_ref, b1_ref, w2_ref, ck_ref, cv_ref, pb_scr):
    n_ch = ksrc_ref.shape[0] // CMP_STRIDE

    def load(j, t, w_ref):
        x = (ksrc_ref, vsrc_ref)[j][pl.ds(t, n_ch, stride=CMP_STRIDE), :]
        return jnp.dot(x.astype(BF16), w_ref[j, t], preferred_element_type=F32)

    for j, out_ref in enumerate((ck_ref, cv_ref)):
        out_ref[0] = _compress_body(load, n_ch, w1ab_ref, pe_ref, w1d_ref, b1_ref, w2_ref, pb_scr, j).astype(BF16)


def _compress(cmp32, consts, batch, seq):
    n_ch = seq // CMP_STRIDE
    full = lambda a: pl.BlockSpec(a.shape, lambda b: (0,) * a.ndim)
    out = pl.BlockSpec((1, n_ch, KV_W), lambda b: (b, 0, 0))
    return pl.pallas_call(
        _compress_kernel,
        name="compress",
        grid=(batch,),
        in_specs=[pl.BlockSpec((seq, KV_W), lambda b: (b, 0)), pl.BlockSpec((seq, KV_W), lambda b: (b, 1))]
        + [full(a) for a in consts],
        out_specs=[out, out],
        out_shape=[jax.ShapeDtypeStruct((batch, n_ch, KV_W), BF16)] * 2,
        scratch_shapes=[pltpu.VMEM((n_ch + 8, KV_W), F32)],
        compiler_params=_cparams("parallel"),
    )(cmp32, cmp32, *consts)


def _cover(n_rows, n_cmp, n_sel, n_cols):
    c_start = np.arange(n_rows)[:, None] * CMP_STRIDE
    s_start = np.arange(n_cols)[None, :] * SEL_BLOCK
    ov = np.clip(np.minimum(c_start + CMP_BLOCK, s_start + SEL_BLOCK) - np.maximum(c_start, s_start), 0, None)
    ov = ov.astype(np.float32) / CMP_BLOCK
    ov[n_cmp:, :] = 0.0
    ov[:, n_sel:] = 0.0
    return jnp.asarray(ov, BF16)


def _sortable(x):
    bits = pltpu.bitcast(x, I32)
    return bits ^ ((bits >> 31) & jnp.int32(0x7FFFFFFF))


def _split2(x):
    hi = x.astype(BF16)
    return hi, (x - hi.astype(F32)).astype(BF16)


def _nt(a, b):
    return lax.dot_general(a, b, (((1,), (1,)), ((), ())), preferred_element_type=F32)


def _flash(q, k_ref, v_ref, lo, hi, bias_fn):
    def body(kb, carry):
        m, l, acc = carry
        r = pl.ds(pl.multiple_of(kb * QB, QB), QB)
        s = bias_fn(kb, _nt(q, k_ref[r, :]))
        m_new = jnp.maximum(m, jnp.max(s, axis=1, keepdims=True))
        alpha = jnp.exp(m - m_new)
        p = jnp.exp(s - m_new)
        l = alpha * l + jnp.sum(p, axis=1, keepdims=True)
        acc = alpha * acc + jnp.dot(p.astype(BF16), v_ref[r, :], preferred_element_type=F32)
        return m_new, l, acc

    init = (jnp.full((QB, 1), NEG, F32), jnp.zeros((QB, 1), F32), jnp.zeros((QB, LANES), F32))
    _, l, acc = lax.fori_loop(lo, hi, body, init)
    return acc / l


def _pair_to_lanes(o_even, o_odd, h_even, lane):
    left = o_even if h_even // GROUP == 0 else pltpu.roll(o_even, HEAD_DIM, 1)
    right = o_odd if (h_even + 1) // GROUP == 1 else pltpu.roll(o_odd, HEAD_DIM, 1)
    return jnp.where(lane < HEAD_DIM, left, right)


def _topn_mask(v, n_cand, n_top, lane):
    rank = jnp.zeros(v.shape, F32)
    for i in range(n_cand):
        ci = v[:, i:i + 1]
        rank = rank + jnp.where((ci > v) | ((ci == v) & (lane > i)), 1.0, 0.0)
    return jnp.where((rank < n_top) & (lane < n_cand), 1.0, 0.0)


def _attn_prompt_kernel(qd_ref, qi_ref, qn_ref, iw_ref, nbg_ref,
                        kd_ref, vd_ref, ki_ref, ks_ref, vs_ref, kw_ref, vw_ref, ck_ref, cv_ref, cover_ref,
                        yb_ref, yc_ref, key_scr, db_scr, sb_scr, acc_scr, *, n_top, n_sel):
    i = pl.program_id(1)
    nkb = i + 1
    row = lax.broadcasted_iota(I32, (QB, LANES), 0)
    lane = lax.broadcasted_iota(I32, (QB, LANES), 1)
    qpos = i * QB + row

    iw = iw_ref[...]

    def idx_blk(kb, c):
        kblk = ki_ref[pl.ds(pl.multiple_of(kb * QB, QB), QB), :]
        score = jnp.zeros((QB, LANES), F32)
        for h in range(IDX_HEADS):
            score = score + iw[:, h:h + 1] * jnp.maximum(_nt(qi_ref[:, h * LANES:(h + 1) * LANES], kblk), 0.0)
        key_scr[kb] = jnp.where(kb * QB + lane <= qpos, _sortable(score), INT_MIN)
        return c

    lax.fori_loop(0, nkb, idx_blk, 0)

    def count(pred):
        acc = lax.fori_loop(0, nkb, lambda kb, a: a + jnp.where(pred(key_scr[kb]), 1.0, 0.0),
                            jnp.zeros((QB, LANES), F32))
        return jnp.sum(acc, axis=1, keepdims=True)

    def bit_step(t, thr):
        cand = thr + jnp.left_shift(jnp.int32(1), 31 - t)
        return jnp.where(count(lambda k: k >= cand) >= n_top, cand, thr)

    thr = lax.fori_loop(0, 32, bit_step, jnp.full((QB, 1), INT_MIN, I32))
    need = n_top - count(lambda k: k > thr)
    upper = jnp.where(row <= lane, 1.0, 0.0).astype(BF16)

    def mask_blk(kb, before):
        key = key_scr[kb]
        eq = key == thr
        pref = jnp.dot(jnp.where(eq, 1.0, 0.0).astype(BF16), upper, preferred_element_type=F32) + before
        sel = ((key > thr) | (eq & (pref <= need))) & (kb * QB + lane <= qpos)
        db_scr[kb] = jnp.where(sel, 0.0, NEG)
        return before + jnp.sum(jnp.where(eq, 1.0, 0.0), axis=1, keepdims=True)

    lax.fori_loop(0, nkb, mask_blk, jnp.zeros((QB, 1), F32))

    for j in range(N_HEADS // 2):
        o = [_flash(qd_ref[:, h * LANES:(h + 1) * LANES], kd_ref, vd_ref, 0, nkb, lambda kb, s: s + db_scr[kb])
             for h in (2 * j, 2 * j + 1)]
        cols = slice(j * LANES, (j + 1) * LANES)
        yb_ref[:, cols] = _pair_to_lanes(o[0], o[1], 2 * j, lane)

    gb = jax.nn.sigmoid(nbg_ref[...])
    gcol = lambda h, r: gb[:, 3 * h + r:3 * h + r + 1]
    ck, cv = ck_ref[0], cv_ref[0]
    n_ch = ck.shape[0]
    cl = lax.broadcasted_iota(I32, (QB, n_ch), 1)
    cvalid = cl * CMP_STRIDE + (CMP_BLOCK - 1) <= i * QB + lax.broadcasted_iota(I32, (QB, n_ch), 0)
    for g in range(KV_HEADS):
        imp = jnp.zeros((QB, n_ch), F32)
        for h in range(g * GROUP, (g + 1) * GROUP):
            s = jnp.where(cvalid, _nt(qn_ref[:, h * LANES:(h + 1) * LANES], ck), NEG)
            e = jnp.exp(s - jnp.max(s, axis=1, keepdims=True))
            p = jnp.where(cvalid, e / jnp.sum(e, axis=1, keepdims=True), 0.0)
            acc_scr[h] = gcol(h, 0) * jnp.dot(p.astype(BF16), cv, preferred_element_type=F32)
            imp = imp + p
        ih, il = _split2(imp)
        impj = (jnp.dot(ih, cover_ref[...], preferred_element_type=F32)
                + jnp.dot(il, cover_ref[...], preferred_element_type=F32))
        tb = qpos >> 6
        forced = (lane == 0) | (lane == tb) | (lane == tb - 1)
        v = jnp.where(lane * SEL_BLOCK <= qpos, jnp.where(forced, FORCE, impj), NEG)
        sel16 = _topn_mask(v, n_sel, min(SEL_TOPN, n_sel), lane).astype(BF16)

        def sb_blk(kb, c, g=g, sel16=sel16):
            expand = jnp.where(row == 2 * kb + (lane >> 6), 1.0, 0.0).astype(BF16)
            tok = jnp.dot(sel16, expand, preferred_element_type=F32)
            sb_scr[g, kb] = jnp.where((tok > 0.5) & (kb * QB + lane <= qpos), 0.0, NEG)
            return c

        lax.fori_loop(0, nkb, sb_blk, 0)

    def win_bias(kb, s):
        kpos = kb * QB + lane
        return jnp.where((kpos <= qpos) & (qpos - kpos < WINDOW), s, NEG)

    for j in range(N_HEADS // 2):
        o = []
        for h in (2 * j, 2 * j + 1):
            q = qn_ref[:, h * LANES:(h + 1) * LANES]
            o_sel = _flash(q, ks_ref, vs_ref, 0, nkb, lambda kb, s, g=h // GROUP: s + sb_scr[g, kb])
            o_win = _flash(q, kw_ref, vw_ref, jnp.maximum(i - WINDOW // QB, 0), nkb, win_bias)
            o.append(acc_scr[h] + gcol(h, 1) * o_sel + gcol(h, 2) * o_win)
        cols = slice(j * LANES, (j + 1) * LANES)
        yc_ref[:, cols] = _pair_to_lanes(o[0], o[1], 2 * j, lane)


def _attn_prompt(qd, qi, qn, iw, z, kv16, ck, cv, batch, seq):
    nq = seq // QB
    n_ch = seq // CMP_STRIDE
    n_sel = seq // SEL_BLOCK
    n_cmp = (seq - CMP_BLOCK) // CMP_STRIDE + 1
    cover = _cover(n_ch, n_cmp, n_sel, LANES)
    qspec = pl.BlockSpec((QB, N_HEADS * LANES), lambda b, i: (b * nq + i, 0))
    zspec = lambda name, w: pl.BlockSpec((QB, w), lambda b, i, o=Z_OFF[name] // w: (b * nq + i, o))
    kvspec = lambda blk: pl.BlockSpec((seq, LANES), lambda b, i: (b, blk))
    cspec = pl.BlockSpec((1, n_ch, KV_W), lambda b, i: (b, 0, 0))
    yspec = pl.BlockSpec((QB, BRANCH_W), lambda b, i: (b * nq + i, 0))
    kern = functools.partial(_attn_prompt_kernel, n_top=min(DSA_TOPK, seq // 4), n_sel=n_sel)
    return pl.pallas_call(
        kern,
        name="attn_prompt",
        grid=(batch, nq),
        in_specs=[qspec, qspec, qspec, pl.BlockSpec((QB, LANES), lambda b, i: (b * nq + i, 0)),
                  zspec("nbg", LANES),
                  kvspec(KV_KD), kvspec(KV_VD), kvspec(KV_KI), kvspec(KV_KS), kvspec(KV_VS), kvspec(KV_KW),
                  kvspec(KV_VW), cspec, cspec, pl.BlockSpec(cover.shape, lambda b, i: (0, 0))],
        out_specs=[yspec, yspec],
        out_shape=[jax.ShapeDtypeStruct((batch * seq, BRANCH_W), F32)] * 2,
        scratch_shapes=[pltpu.VMEM((nq, QB, LANES), I32), pltpu.VMEM((nq, QB, LANES), F32),
                        pltpu.VMEM((KV_HEADS, nq, QB, LANES), F32), pltpu.VMEM((N_HEADS, QB, LANES), F32)],
        compiler_params=_cparams("parallel", "arbitrary"),
    )(qd, qi, qn, iw, z, kv16, kv16, kv16, kv16, kv16, kv16, kv16, ck, cv, cover)


C_DK, C_DV, C_IK, C_CKV, C_CVS, C_SKV = range(6)
N_CBLK = 6


def _key_chunk(n_blocks):
    return QB * max(d for d in range(1, 17) if n_blocks % d == 0)


def _attn_sample_kernel(pt_ref, cache_ref, new_ref, qd_ref, qi_ref, qn_ref, iw_ref, nbg_ref, win_ref,
                        w1s_ref, pe_ref, w1d_ref, b1_ref, w2_ref, cover_ref,
                        od_ref, oc_ref, buf, tail, pb_scr, tok_scr, pref_scr, sem, *, layer, past, n_top):
    b = pl.program_id(0)
    n_pages = past // PAGE_SIZE
    nk = past + QB
    kc = _key_chunk(nk // QB)
    n_ch = past // CMP_STRIDE
    n_sel = -(-(past + 1) // SEL_BLOCK)
    n_selp = cover_ref.shape[1]

    def page_copies(p):
        src = cache_ref.at[layer, pt_ref[b, p]]
        rows = pl.ds(pl.multiple_of(p * PAGE_SIZE, PAGE_SIZE), PAGE_SIZE)
        cps = [pltpu.make_async_copy(src.at[:, pl.ds(j * LANES, LANES)], buf.at[j, rows, :], sem)
               for j in range(N_CBLK)]
        return cps + [pltpu.make_async_copy(src.at[:, pl.ds(N_CBLK * LANES, HEAD_DIM)], tail.at[rows, :], sem)]

    def start(p, c):
        for cp in page_copies(p):
            cp.start()
        return c

    def wait(p, c):
        for cp in page_copies(p):
            cp.wait()
        return c

    lax.fori_loop(0, n_pages, start, 0)
    new = new_ref[0]
    for j in range(N_CBLK):
        buf[j, pl.ds(past, QB), :] = jnp.zeros((QB, LANES), F32)
        buf[j, pl.ds(past, 1), :] = new[:, j * LANES:(j + 1) * LANES]
    tail[pl.ds(past, QB), :] = jnp.zeros((QB, HEAD_DIM), F32)
    tail[pl.ds(past, 1), :] = new[:, N_CBLK * LANES:]
    lax.fori_loop(0, n_pages, wait, 0)

    row8 = lax.broadcasted_iota(I32, (N_HEADS, nk), 0)
    kpos1 = lax.broadcasted_iota(I32, (1, nk), 1)
    in_range = kpos1 <= past
    lo_heads = lax.broadcasted_iota(I32, (N_HEADS, HEAD_DIM), 0) < GROUP

    def scores(q, blk):
        return jnp.concatenate([_nt(q, buf[blk, c:c + kc, :]) for c in range(0, nk, kc)], axis=1)

    def weighted(p, blk):
        vals = (lambda c: tail[c:c + kc, :]) if blk is None else (lambda c: buf[blk, c:c + kc, :])
        return sum(jnp.dot(p[:, c:c + kc], vals(c), preferred_element_type=F32) for c in range(0, nk, kc))

    def softmax(s):
        e = jnp.exp(s - jnp.max(s, axis=1, keepdims=True))
        return e / jnp.sum(e, axis=1, keepdims=True)

    compact = lambda o: jnp.where(lo_heads, o[:, :HEAD_DIM], o[:, HEAD_DIM:])

    sc = jnp.sum(iw_ref[0] * jnp.maximum(scores(qi_ref[0], C_IK), 0.0), axis=0, keepdims=True)
    key = jnp.where(in_range, _sortable(sc), INT_MIN)
    count = lambda pred: jnp.sum(jnp.where(pred, 1.0, 0.0), axis=1, keepdims=True)

    def bit_step(t, thr):
        cand = thr + jnp.left_shift(jnp.int32(1), 31 - t)
        return jnp.where(count(key >= cand) >= n_top, cand, thr)

    thr = lax.fori_loop(0, 32, bit_step, jnp.full((1, 1), INT_MIN, I32))
    need = n_top - count(key > thr)
    eq8 = jnp.where(jnp.broadcast_to(key == thr, (N_HEADS, nk)), 1.0, 0.0)
    upper = jnp.where(lax.broadcasted_iota(I32, (LANES, LANES), 0) <= lax.broadcasted_iota(I32, (LANES, LANES), 1),
                      1.0, 0.0).astype(BF16)
    before = jnp.zeros((N_HEADS, 1), F32)
    for c in range(0, nk, LANES):
        pref = jnp.dot(eq8[:, c:c + LANES].astype(BF16), upper, preferred_element_type=F32) + before
        pref_scr[:, c:c + LANES] = pref
        before = before + jnp.sum(eq8[:, c:c + LANES], axis=1, keepdims=True)
    sel = ((key > thr) | ((key == thr) & (pref_scr[0:1, :] <= need))) & in_range
    p = softmax(scores(qd_ref[0], C_DK) + jnp.where(sel, 0.0, NEG))
    od_ref[0] = compact(weighted(p, C_DV))

    def load(j, t, w_ref):
        blks = (C_IK, C_CKV) if j == 0 else (C_CKV, C_CVS)
        return sum(jnp.dot(buf[blk, pl.ds(t, n_ch, stride=CMP_STRIDE), :].astype(BF16), w_ref[j, i, t],
                           preferred_element_type=F32) for i, blk in enumerate(blks))

    ck = _compress_body(load, n_ch, w1s_ref, pe_ref, w1d_ref, b1_ref, w2_ref, pb_scr, 0).astype(BF16)
    cv = _compress_body(load, n_ch, w1s_ref, pe_ref, w1d_ref, b1_ref, w2_ref, pb_scr, 1).astype(BF16)

    gb = jax.nn.sigmoid(nbg_ref[0])
    qn = qn_ref[0]
    cvalid = lax.broadcasted_iota(I32, (N_HEADS, n_ch), 1) * CMP_STRIDE + (CMP_BLOCK - 1) <= past
    s = jnp.where(cvalid, _nt(qn.astype(BF16), ck), NEG)
    e = jnp.exp(s - jnp.max(s, axis=1, keepdims=True))
    pc = jnp.where(cvalid, e / jnp.sum(e, axis=1, keepdims=True), 0.0)
    o_cmp = compact(jnp.dot(pc.astype(BF16), cv, preferred_element_type=F32))
    imp = jnp.where(lax.broadcasted_iota(I32, (N_HEADS, n_ch), 0) < GROUP,
                    jnp.sum(pc[:GROUP], axis=0, keepdims=True), jnp.sum(pc[GROUP:], axis=0, keepdims=True))
    ih, il = _split2(imp)
    impj = (jnp.dot(ih, cover_ref[...], preferred_element_type=F32)
            + jnp.dot(il, cover_ref[...], preferred_element_type=F32))
    lane_s = lax.broadcasted_iota(I32, (N_HEADS, n_selp), 1)
    tb = past // SEL_BLOCK
    forced = (lane_s == 0) | (lane_s == tb) | (lane_s == tb - 1)
    v = jnp.where(lane_s * SEL_BLOCK <= past, jnp.where(forced, FORCE, impj), NEG)
    sel8 = _topn_mask(v, n_sel, min(SEL_TOPN, n_sel), lane_s)
    half = lax.broadcasted_iota(I32, (N_HEADS, LANES), 1) < SEL_BLOCK
    for c in range(nk // LANES):
        tok_scr[:, c * LANES:(c + 1) * LANES] = jnp.where(half, sel8[:, 2 * c:2 * c + 1], sel8[:, 2 * c + 1:2 * c + 2])

    qsw = pltpu.roll(qn, HEAD_DIM, 1)
    s = jnp.where(row8 < GROUP, scores(qsw, C_CVS), scores(qsw, C_SKV))
    p = softmax(jnp.where((tok_scr[...] > 0.5) & in_range, s, NEG))
    o_sel = jnp.where(lo_heads, weighted(p, C_SKV)[:, HEAD_DIM:], weighted(p, None))

    pw = softmax(_nt(qn, win_ref[0, :, 0:KV_W]))
    o_win = compact(jnp.dot(pw, win_ref[0, :, KV_W:2 * KV_W], preferred_element_type=F32))
    oc_ref[0] = gb[:, 0:1] * o_cmp + gb[:, 1:2] * o_sel + gb[:, 2:3] * o_win


def _compress_consts_sample(w1ab):
    sw = jnp.concatenate([w1ab[:, :, HEAD_DIM:], w1ab[:, :, :HEAD_DIM]], axis=2)
    upper = jnp.concatenate([jnp.zeros_like(sw[:, :, :HEAD_DIM]), sw[:, :, HEAD_DIM:]], axis=2)
    lower = jnp.concatenate([sw[:, :, :HEAD_DIM], jnp.zeros_like(sw[:, :, HEAD_DIM:])], axis=2)
    return jnp.stack([upper, lower], axis=1)


def _attn_sample(page_table, cache_kv, layer, new_rows, qd, qi, qn, iw, nbg, win, cmp_consts):
    nb, n_pages = page_table.shape
    past = n_pages * PAGE_SIZE
    nk = past + QB
    n_ch = past // CMP_STRIDE
    n_sel = -(-(past + 1) // SEL_BLOCK)
    n_selp = -(-n_sel // LANES) * LANES
    n_cmp = (past + 1 - CMP_BLOCK) // CMP_STRIDE + 1
    cover = _cover(n_ch, n_cmp, n_sel, n_selp)
    w1ab, pe8, w1d, b1d, w2bd = cmp_consts
    w1s = _compress_consts_sample(w1ab)
    consts = (w1s, pe8, w1d, b1d, w2bd, cover)
    per_seq = lambda a: pl.BlockSpec((1,) + a.shape[1:], lambda b, pt: (b,) + (0,) * (a.ndim - 1))
    full = lambda a: pl.BlockSpec(a.shape, lambda b, pt: (0,) * a.ndim)
    args = (new_rows, qd, qi, qn, iw, nbg, win)
    kern = functools.partial(_attn_sample_kernel, layer=layer, past=past, n_top=min(DSA_TOPK, (past + 1) // 4))
    out = pl.BlockSpec((1, N_HEADS, HEAD_DIM), lambda b, pt: (b, 0, 0))
    return pl.pallas_call(
        kern,
        name="attn_sample",
        grid_spec=pltpu.PrefetchScalarGridSpec(
            num_scalar_prefetch=1,
            grid=(nb,),
            in_specs=[pl.BlockSpec(memory_space=pl.ANY)] + [per_seq(a) for a in args] + [full(a) for a in consts],
            out_specs=[out, out],
            scratch_shapes=[pltpu.VMEM((N_CBLK, nk, LANES), F32), pltpu.VMEM((nk, HEAD_DIM), F32),
                            pltpu.VMEM((n_ch + 8, KV_W), F32),
                            pltpu.VMEM((N_HEADS, nk), F32), pltpu.VMEM((N_HEADS, nk), F32),
                            pltpu.SemaphoreType.DMA(())]),
        out_shape=[jax.ShapeDtypeStruct((nb, N_HEADS, HEAD_DIM), F32)] * 2,
        compiler_params=_cparams("arbitrary"),
    )(page_table, cache_kv, *args, *consts)


def _merge_kernel(x_ref, ya_ref, yb_ref, yc_ref, dg_ref, ng_ref, ga_ref, gb_ref, gc_ref, wb_ref, wo_ref, y_ref):
    silu = lambda v: v * jax.nn.sigmoid(v)
    ys = (ya_ref[...], yb_ref[...] * silu(dg_ref[...]), yc_ref[...] * silu(ng_ref[...]))
    m = None
    for n, (y, g) in enumerate(zip(ys, (ga_ref, gb_ref, gc_ref))):
        pr = jax.nn.sigmoid(g[...]) * jnp.dot(y.astype(BF16), wb_ref[n], preferred_element_type=F32)
        m = pr if m is None else m + pr
    y_ref[...] = x_ref[...] + jnp.dot(m.astype(BF16), wo_ref[...], preferred_element_type=F32)


def _merge(x2, ya, yb, yc, z, wb16, wo16):
    t, d = x2.shape
    tm = min(t, 512)
    yspec = pl.BlockSpec((tm, BRANCH_W), lambda i: (i, 0))
    gspec = lambda name, w=d: pl.BlockSpec((tm, w), lambda i, o=Z_OFF[name] // w: (i, o))
    return pl.pallas_call(
        _merge_kernel,
        name="merge",
        grid=(t // tm,),
        in_specs=[pl.BlockSpec((tm, d), lambda i: (i, 0)), yspec, yspec, yspec,
                  gspec("dgate", BRANCH_W), gspec("ngate", BRANCH_W), gspec("ma"), gspec("mb"), gspec("mc"),
                  pl.BlockSpec(wb16.shape, lambda i: (0, 0, 0)), pl.BlockSpec(wo16.shape, lambda i: (0, 0))],
        out_specs=pl.BlockSpec((tm, d), lambda i: (i, 0)),
        out_shape=jax.ShapeDtypeStruct((t, d), F32),
        compiler_params=_cparams("parallel"),
    )(x2, ya, yb, yc, z, z, z, z, z, wb16, wo16)


def _layer_prompt(x, lb, norm_g, w16, gon, prep_consts, cmp_consts, wb16, wo16):
    batch, seq, d = x.shape
    x2 = x.reshape(batch * seq, d)
    z = _proj(x2, norm_g, w16)
    tabs = _rope_tables(jnp.arange(seq, dtype=I32))
    rows, wrows, cmp32, qd, qi, qn, iw, kv16 = _prep(z, tabs, *prep_consts, min(seq, 256))
    ya, s_new = _hgrn_prompt(z, lb, gon, batch, seq)
    ck, cv = _compress(cmp32, cmp_consts, batch, seq)
    yb, yc = _attn_prompt(qd, qi, qn, iw, z, kv16, ck, cv, batch, seq)
    y = _merge(x2, ya, yb, yc, z, wb16, wo16).reshape(batch, seq, d)
    wrows = wrows.reshape(batch, seq, WIN_W)
    return y, rows.reshape(batch, seq, CACHE_W), wrows[:, -min(WINDOW, seq):], s_new


def _layer_sample(x, lb, norm_g, w16, gon, prep_consts, cmp_consts, wb16, wo16,
                  page_table, cache_kv, layer, win_rows, s0):
    nb, one, d = x.shape
    assert one == 1 and win_rows.shape[1] == WINDOW
    past = page_table.shape[1] * PAGE_SIZE
    x2 = x.reshape(nb, d)
    z = _proj(x2, norm_g, w16)
    tabs = _rope_tables(jnp.full((nb,), past, I32))
    rows, wrows, _, qd, qi, qn, iw, _ = _prep(z, tabs, *prep_consts, nb)
    ya, s_new = _hgrn_sample(z, lb, gon, s0)
    win = jnp.concatenate([win_rows[:, 1:], wrows[:, None, :]], axis=1)
    heads = lambda q: q.astype(F32).reshape(nb, N_HEADS, LANES)
    nbg = z[:, Z_OFF["nbg"]:Z_OFF["nbg"] + 3 * N_HEADS].reshape(nb, N_HEADS, 3)
    od, oc = _attn_sample(page_table, cache_kv, layer, rows.reshape(nb, 1, CACHE_W), heads(qd), heads(qi), heads(qn),
                          iw[:, :IDX_HEADS].reshape(nb, IDX_HEADS, 1), nbg, win, cmp_consts)
    y = _merge(x2, ya, od.reshape(nb, BRANCH_W), oc.reshape(nb, BRANCH_W), z, wb16, wo16)
    return y.reshape(nb, 1, d), rows.reshape(nb, 1, CACHE_W), win, s_new


def kernel(x_prompt, x_sample, cache_kv, page_table, state_win, state_hgrn, norm_g, w_in,
           hgrn_lb_logits, hgrn_onorm_g, dsa_qk_norm_g, dsa_idx_k_norm_g, nsa_qk_norm_g,
           nsa_cmp_pe, nsa_cmp_w1, nsa_cmp_b1, nsa_cmp_w2, w_branch, w_out):
    lb_soft = jax.nn.softmax(hgrn_lb_logits.astype(F32), axis=0)
    lbs = jnp.cumsum(lb_soft, axis=0) - lb_soft[0]
    xp, xs = x_prompt, x_sample
    outs = [[] for _ in range(6)]
    for l in range(w_in.shape[0]):
        shared = (lbs[l], norm_g[l], _perm_w_in(w_in[l]), hgrn_onorm_g[l],
                  _prep_consts(dsa_qk_norm_g[l], dsa_idx_k_norm_g[l], nsa_qk_norm_g[l]),
                  _compress_consts(nsa_cmp_pe[l], nsa_cmp_w1[l], nsa_cmp_b1[l], nsa_cmp_w2[l]),
                  w_branch[l].astype(BF16), w_out[l].astype(BF16))
        xs, *s_out = _layer_sample(xs, *shared, page_table, cache_kv, l, state_win[l], state_hgrn[l])
        xp, *p_out = _layer_prompt(xp, *shared)
        for acc, o in zip(outs, p_out + s_out):
            acc.append(o)
    return (xp, xs) + tuple(jnp.stack(o) for o in outs)
```

```python
import functools

import jax
import jax.numpy as jnp
from jax import lax
import numpy as np
from jax.experimental import pallas as pl
from jax.experimental.pallas import tpu as pltpu

F32 = jnp.float32
BF16 = jnp.bfloat16
I32 = jnp.int32

D_MODEL = 1024
PAGE_SIZE = 128
BRANCH_W = D_MODEL // 2
HEAD_DIM = 64
ROPE_THETA = 500000.0
EPS = 1e-6
NEG = -1e30
FORCE = 1e9
HG_HEADS = 4
HG_K = BRANCH_W // HG_HEADS
HG_V = BRANCH_W // HG_HEADS
HG_CHUNK = 64
N_HEADS = BRANCH_W // HEAD_DIM
KV_HEADS = 2
GROUP = N_HEADS // KV_HEADS
IDX_HEADS = 8
IDX_DIM = 64
DSA_TOPK = 256
CMP_BLOCK = 32
CMP_STRIDE = 16
SEL_BLOCK = 64
SEL_TOPN = 16
WINDOW = 512
KV_W = KV_HEADS * HEAD_DIM
CACHE_W = 2 * KV_W + IDX_DIM + 4 * KV_W
WIN_W = 2 * KV_W
LANES = 128
QB = 128
INT_MIN = np.int32(-2 ** 31)
VMEM_LIMIT = 56 * 1024 * 1024

IN_SPLITS = (
    ("hq", 512), ("hf", 512), ("hi", 512), ("hgate", 512),
    ("dq", 512), ("dk", 128), ("dv", 128), ("iq", 512), ("ik", 64), ("iw", 8), ("dgate", 512),
    ("nq", 512), ("nck", 128), ("ncv", 128), ("nsk", 128), ("nsv", 128), ("nwk", 128), ("nwv", 128),
    ("nbg", 24), ("ngate", 512), ("ma", 1024), ("mb", 1024), ("mc", 1024),
)
Z_ORDER = ("ma", "mb", "mc", "hq", "hf", "hi", "hgate", "dq", "iq", "nq", "dgate", "ngate",
           "dk", "dv", "ik", "iw", "nck", "ncv", "nsk", "nsv", "nwk", "nwv", "nbg")
Z_COLS = 9216


def _z_layout():
    widths = dict(IN_SPLITS)
    src, off = {}, 0
    for name, w in IN_SPLITS:
        src[name] = off
        off += w
    dst, off = {}, 0
    for name in Z_ORDER:
        dst[name] = off
        off += -(-widths[name] // LANES) * LANES
    assert off <= Z_COLS
    idx = np.zeros((Z_COLS,), np.int32)
    valid = np.zeros((Z_COLS,), np.float32)
    for name in Z_ORDER:
        w = widths[name]
        idx[dst[name]:dst[name] + w] = src[name] + np.arange(w)
        valid[dst[name]:dst[name] + w] = 1.0
    return dst, idx, valid


Z_OFF, Z_SRC_IDX, Z_VALID = _z_layout()


def _cparams(*sem):
    return pltpu.CompilerParams(dimension_semantics=sem, vmem_limit_bytes=VMEM_LIMIT)


def _proj_kernel(x_ref, g_ref, w_ref, z_ref, h_scr):
    @pl.when(pl.program_id(1) == 0)
    def _():
        x = x_ref[...]
        ms = jnp.mean(x * x, axis=-1, keepdims=True)
        h_scr[...] = (x * lax.rsqrt(ms + EPS) * g_ref[...]).astype(BF16)

    z_ref[...] = jnp.dot(h_scr[...], w_ref[...], preferred_element_type=F32)


def _proj(x2, g, w16):
    t, d = x2.shape
    tm = min(t, 512)
    tn = 1536
    return pl.pallas_call(
        _proj_kernel,
        name="proj",
        grid=(t // tm, Z_COLS // tn),
        in_specs=[pl.BlockSpec((tm, d), lambda i, j: (i, 0)),
                  pl.BlockSpec((1, d), lambda i, j: (0, 0)),
                  pl.BlockSpec((d, tn), lambda i, j: (0, j))],
        out_specs=pl.BlockSpec((tm, tn), lambda i, j: (i, j)),
        out_shape=jax.ShapeDtypeStruct((t, Z_COLS), F32),
        scratch_shapes=[pltpu.VMEM((tm, d), BF16)],
        compiler_params=_cparams("parallel", "arbitrary"),
    )(x2, g.reshape(1, d), w16)


KV_KD, KV_VD, KV_KI, KV_KS, KV_VS, KV_KW, KV_VW = range(7)
G_DQ, G_DK, G_IK, G_NQ, G_NCK, G_NSK, G_NWK = range(7)


def _head_rms(x, gain, bd):
    x2 = x * x
    hi = x2.astype(BF16)
    lo = (x2 - hi.astype(F32)).astype(BF16)
    ms = jnp.dot(hi, bd, preferred_element_type=F32) + jnp.dot(lo, bd, preferred_element_type=F32)
    return x * lax.rsqrt(ms + EPS) * gain


def _rope128(x, cosf, sina, sinb):
    return x * cosf + pltpu.roll(x, LANES - 8, 1) * sina + pltpu.roll(x, 8, 1) * sinb


def _prep_kernel(dq_ref, iq_ref, nq_ref, s0_ref, s1_ref, s2_ref, cos_ref, sina_ref, sinb_ref, g_ref, bd_ref,
                 rows_ref, wrows_ref, cmp_ref, qd_ref, qi_ref, qn_ref, iw_ref, kv_ref):
    cosf, sina, sinb = cos_ref[...], sina_ref[...], sinb_ref[...]
    bd = bd_ref[...]
    lane = lax.broadcasted_iota(I32, cosf.shape, 1)
    lo_half = lane < HEAD_DIM
    rope = lambda v: _rope128(v, cosf, sina, sinb)
    gain = lambda r: g_ref[r:r + 1, :]

    def put_q(src_ref, out_ref, g_row, scale, target_of_head):
        for j in range(N_HEADS // 2):
            x = src_ref[:, j * LANES:(j + 1) * LANES]
            if g_row is not None:
                x = _head_rms(x, gain(g_row), bd)
            x = rope(x) * scale
            xr = pltpu.roll(x, HEAD_DIM, 1)
            for half in range(2):
                h = 2 * j + half
                tgt = target_of_head(h)
                v = x if tgt == half else xr
                keep = lo_half if tgt == 0 else jnp.logical_not(lo_half)
                out_ref[:, h * LANES:(h + 1) * LANES] = jnp.where(keep, v, 0.0).astype(BF16)

    put_q(dq_ref, qd_ref, G_DQ, HEAD_DIM ** -0.5, lambda h: h // GROUP)
    put_q(iq_ref, qi_ref, None, 1.0, lambda h: 0)
    put_q(nq_ref, qn_ref, G_NQ, HEAD_DIM ** -0.5, lambda h: h // GROUP)

    dk = rope(_head_rms(s0_ref[:, 0:128], gain(G_DK), bd))
    dv = s0_ref[:, 128:256]
    ik = rope(_head_rms(s0_ref[:, 256:384], gain(G_IK), bd))
    iw_ref[...] = s0_ref[:, 384:512] * (IDX_HEADS ** -0.5 * IDX_DIM ** -0.5)
    nck = rope(_head_rms(s1_ref[:, 0:128], gain(G_NCK), bd))
    ncv = s1_ref[:, 128:256]
    nsk = rope(_head_rms(s1_ref[:, 256:384], gain(G_NSK), bd))
    nsv = s1_ref[:, 384:512]
    nwk = rope(_head_rms(s2_ref[:, 0:128], gain(G_NWK), bd))
    nwv = s2_ref[:, 128:256]

    rows_ref[:, 0:128] = dk
    rows_ref[:, 128:256] = dv
    rows_ref[:, 256:320] = ik[:, 0:64]
    rows_ref[:, 320:448] = nck
    rows_ref[:, 448:576] = ncv
    rows_ref[:, 576:704] = nsk
    rows_ref[:, 704:832] = nsv
    wrows_ref[:, 0:128] = nwk
    wrows_ref[:, 128:256] = nwv
    cmp_ref[:, 0:128] = nck
    cmp_ref[:, 128:256] = ncv
    for blk, v in ((KV_KD, dk), (KV_VD, dv), (KV_KI, ik), (KV_KS, nsk), (KV_VS, nsv), (KV_KW, nwk), (KV_VW, nwv)):
        kv_ref[:, blk * LANES:(blk + 1) * LANES] = v.astype(BF16)


def _prep(z, tabs, gains, bd, tm):
    t = z.shape[0]
    nt = tabs[0].shape[0] // tm
    zb = lambda blk: pl.BlockSpec((tm, 512), lambda i: (i, blk))
    tab = pl.BlockSpec((tm, LANES), lambda i: (i % nt, 0))
    full = lambda a: pl.BlockSpec(a.shape, lambda i: (0,) * a.ndim)
    outs = [(CACHE_W, F32), (WIN_W, F32), (2 * KV_W, F32), (N_HEADS * LANES, BF16), (N_HEADS * LANES, BF16),
            (N_HEADS * LANES, BF16), (LANES, F32), (7 * LANES, BF16)]
    return pl.pallas_call(
        _prep_kernel,
        name="prep",
        grid=(t // tm,),
        in_specs=[zb(Z_OFF["dq"] // 512), zb(Z_OFF["iq"] // 512), zb(Z_OFF["nq"] // 512),
                  zb(Z_OFF["dk"] // 512), zb(Z_OFF["nck"] // 512), zb(Z_OFF["nwk"] // 512),
                  tab, tab, tab, full(gains), full(bd)],
        out_specs=[pl.BlockSpec((tm, w), lambda i: (i, 0)) for w, _ in outs],
        out_shape=[jax.ShapeDtypeStruct((t, w), dt) for w, dt in outs],
        compiler_params=_cparams("parallel"),
    )(z, z, z, z, z, z, *tabs, gains, bd)


def _rope_tables(pos):
    half = HEAD_DIM // 8
    inv = ROPE_THETA ** (-jnp.arange(half, dtype=F32) / half)
    ang = pos.astype(F32)[:, None] * inv[None, :]
    cos, sin = jnp.cos(ang), jnp.sin(ang)
    n = pos.shape[0]
    one = jnp.ones((n, HEAD_DIM - 2 * half), F32)
    zero = jnp.zeros((n, HEAD_DIM - 2 * half), F32)
    z8 = jnp.zeros((n, half), F32)
    cosf = jnp.concatenate([cos, cos, one], axis=1)
    sina = jnp.concatenate([-sin, z8, zero], axis=1)
    sinb = jnp.concatenate([z8, sin, zero], axis=1)
    return tuple(jnp.tile(a, (1, 2)) for a in (cosf, sina, sinb))


def _perm_w_in(w_in):
    return (jnp.take(w_in, jnp.asarray(Z_SRC_IDX), axis=1) * jnp.asarray(Z_VALID)[None, :]).astype(BF16)


def _prep_consts(dsa_qk_g, idx_k_g, nsa_qk_g):
    two = lambda g: jnp.tile(g.astype(F32), 2)
    ikg = jnp.concatenate([idx_k_g.astype(F32), jnp.ones((HEAD_DIM,), F32)])
    rows = [two(dsa_qk_g[0]), two(dsa_qk_g[1]), ikg, two(nsa_qk_g[0]), two(nsa_qk_g[1]), two(nsa_qk_g[2]),
            two(nsa_qk_g[3]), jnp.ones((LANES,), F32)]
    bd = np.kron(np.eye(2, dtype=np.float32), np.full((HEAD_DIM, HEAD_DIM), 1.0 / HEAD_DIM, np.float32))
    return jnp.stack(rows), jnp.asarray(bd, BF16)


def _split3(x):
    a = x.astype(BF16)
    r = x - a.astype(F32)
    b = r.astype(BF16)
    c = (r - b.astype(F32)).astype(BF16)
    return a, b, c


def _row_to_col(row, eye):
    return jnp.sum(jnp.where(eye, row, 0.0), axis=1, keepdims=True)


def _hgrn_gates(fr, lb):
    sig = jax.nn.sigmoid(fr)
    f = lb + (1.0 - lb) * sig
    return jnp.log(f), (1.0 - lb) * jax.nn.sigmoid(-fr)


def _hgrn_out(o, gon, gate):
    ms = jnp.mean(o * o, axis=-1, keepdims=True)
    return o * lax.rsqrt(ms + EPS) * gon * (gate * jax.nn.sigmoid(gate))


def _hgrn_prompt_kernel(hq_ref, hf_ref, hi_ref, hg_ref, lb_ref, gon_ref, ya_ref, s_ref, b_scr, k_scr):
    c = HG_CHUNK
    n_chunks = hq_ref.shape[0] // c
    lb, gon = lb_ref[...], gon_ref[...]
    ti = lax.broadcasted_iota(I32, (c, HG_K), 0)
    tril = (lax.broadcasted_iota(I32, (c, c), 0) >= lax.broadcasted_iota(I32, (c, c), 1)).astype(BF16)
    lane_c = lax.broadcasted_iota(I32, (c, c), 1)
    eye = lax.broadcasted_iota(I32, (HG_K, HG_K), 0) == lax.broadcasted_iota(I32, (HG_K, HG_K), 1)
    s_ref[0, 0] = jnp.zeros((HG_K, HG_V), F32)

    def chunk(ci, carry):
        rows = pl.ds(pl.multiple_of(ci * c, c), c)
        g, kk = _hgrn_gates(hf_ref[rows, :], lb)
        hq = hq_ref[rows, :]
        qq = hq * jax.nn.sigmoid(hq)
        v = hi_ref[rows, :]
        b = sum(jnp.dot(tril, part, preferred_element_type=F32) for part in _split3(g))
        b_scr[...] = b
        k_scr[...] = kk
        a_tiles = []
        for r0 in range(0, c, 8):
            bt, qt = b[r0:r0 + 8], qq[r0:r0 + 8]
            at = jnp.zeros((8, c), F32)
            for s in range(r0 + 8):
                diff = bt - b_scr[s:s + 1, :]
                if s >= r0:
                    diff = jnp.where(ti[0:8] + r0 >= s, diff, -jnp.inf)
                colv = jnp.sum(qt * jnp.exp(diff) * k_scr[s:s + 1, :], axis=1, keepdims=True)
                at = jnp.where(lane_c[0:8] == s, colv, at)
            a_tiles.append(at)
        a = jnp.concatenate(a_tiles, axis=0)
        s_old = s_ref[0, 0]
        o = (jnp.dot(a.astype(BF16), v.astype(BF16), preferred_element_type=F32)
             + jnp.dot((qq * jnp.exp(b)).astype(BF16), s_old.astype(BF16), preferred_element_type=F32))
        bl = b[c - 1:c, :]
        kdec = (kk * jnp.exp(bl - b)).astype(BF16)
        s_ref[0, 0] = (_row_to_col(jnp.exp(bl), eye) * s_old
                       + lax.dot_general(kdec, v.astype(BF16), (((0,), (0,)), ((), ())),
                                         preferred_element_type=F32))
        ya_ref[rows, :] = _hgrn_out(o, gon, hg_ref[rows, :])
        return carry

    lax.fori_loop(0, n_chunks, chunk, 0)


def _hgrn_prompt(z, lb, gon, batch, seq):
    zb = lambda name: pl.BlockSpec((seq, HG_K), lambda b, h, o=Z_OFF[name] // HG_K: (b, o + h))
    vec = pl.BlockSpec((1, HG_K), lambda b, h: (0, h))
    return pl.pallas_call(
        _hgrn_prompt_kernel,
        name="hgrn_prompt",
        grid=(batch, HG_HEADS),
        in_specs=[zb("hq"), zb("hf"), zb("hi"), zb("hgate"), vec, vec],
        out_specs=[pl.BlockSpec((seq, HG_V), lambda b, h: (b, h)),
                   pl.BlockSpec((1, 1, HG_K, HG_V), lambda b, h: (b, h, 0, 0))],
        out_shape=[jax.ShapeDtypeStruct((batch * seq, BRANCH_W), F32),
                   jax.ShapeDtypeStruct((batch, HG_HEADS, HG_K, HG_V), F32)],
        scratch_shapes=[pltpu.VMEM((HG_CHUNK, HG_K), F32), pltpu.VMEM((HG_CHUNK, HG_K), F32)],
        compiler_params=_cparams("parallel", "parallel"),
    )(z, z, z, z, lb.reshape(1, BRANCH_W), gon.reshape(1, BRANCH_W))


def _hgrn_sample_kernel(hq_ref, hf_ref, hi_ref, hg_ref, lb_ref, gon_ref, s0_ref, ya_ref, s_ref):
    n = hq_ref.shape[0]
    lb, gon = lb_ref[...], gon_ref[...]
    eye = lax.broadcasted_iota(I32, (HG_K, HG_K), 0) == lax.broadcasted_iota(I32, (HG_K, HG_K), 1)

    def seq(b, carry):
        row = pl.ds(b, 1)
        g, kk = _hgrn_gates(hf_ref[row, :], lb)
        hq = hq_ref[row, :]
        qq = hq * jax.nn.sigmoid(hq)
        s_new = (_row_to_col(jnp.exp(g), eye) * s0_ref[b, 0] + _row_to_col(kk, eye) * hi_ref[row, :])
        s_ref[b, 0] = s_new
        o = jnp.sum(_row_to_col(qq, eye) * s_new, axis=0, keepdims=True)
        ya_ref[row, :] = _hgrn_out(o, gon, hg_ref[row, :])
        return carry

    lax.fori_loop(0, n, seq, 0)


def _hgrn_sample(z, lb, gon, s0):
    n = z.shape[0]
    zb = lambda name: pl.BlockSpec((n, HG_K), lambda h, o=Z_OFF[name] // HG_K: (0, o + h))
    vec = pl.BlockSpec((1, HG_K), lambda h: (0, h))
    st = pl.BlockSpec((n, 1, HG_K, HG_V), lambda h: (0, h, 0, 0))
    return pl.pallas_call(
        _hgrn_sample_kernel,
        name="hgrn_sample",
        grid=(HG_HEADS,),
        in_specs=[zb("hq"), zb("hf"), zb("hi"), zb("hgate"), vec, vec, st],
        out_specs=[pl.BlockSpec((n, HG_V), lambda h: (0, h)), st],
        out_shape=[jax.ShapeDtypeStruct((n, BRANCH_W), F32),
                   jax.ShapeDtypeStruct((n, HG_HEADS, HG_K, HG_V), F32)],
        compiler_params=_cparams("parallel"),
    )(z, z, z, z, lb.reshape(1, BRANCH_W), gon.reshape(1, BRANCH_W), s0)


def _compress_consts(pe, w1, b1, w2):
    eye2 = jnp.eye(KV_HEADS, dtype=F32)
    w1r = w1.reshape(2, CMP_BLOCK, HEAD_DIM, HEAD_DIM)
    kron = lambda a: jnp.einsum('ij,ktdh->ktidjh', eye2, a).reshape(2, CMP_STRIDE, KV_W, KV_W)
    w1ab = jnp.concatenate([kron(w1r[:, :CMP_STRIDE]), kron(w1r[:, CMP_STRIDE:])], axis=-1).astype(BF16)
    pe8 = jnp.zeros((2, 8, CMP_BLOCK * HEAD_DIM), F32).at[:, 0].set(pe.reshape(2, -1)).astype(BF16)
    w1d = jnp.tile(w1, (1, 1, KV_HEADS)).astype(BF16)
    b1d = jnp.tile(b1, (1, KV_HEADS)).reshape(2, 1, KV_W).astype(F32)
    w2bd = jnp.einsum('ij,kdh->kidjh', eye2, w2).reshape(2, KV_W, KV_W).astype(BF16)
    return w1ab, pe8, w1d, b1d, w2bd


def _compress_body(load_chunk_rows, n_ch, w1ab_ref, pe_ref, w1d_ref, b1_ref, w2_ref, pb_scr, j):
    acc = jnp.zeros((n_ch, 2 * KV_W), F32)
    for t in range(CMP_STRIDE):
        acc = acc + load_chunk_rows(j, t, w1ab_ref)
    c0 = b1_ref[j] + jnp.dot(pe_ref[j], w1d_ref[j], preferred_element_type=F32)[0:1, :]
    pb_scr[0:n_ch, :] = acc[:, KV_W:]
    pb_scr[n_ch:n_ch + 8, :] = jnp.zeros((8, KV_W), F32)
    hid = acc[:, :KV_W] + pb_scr[pl.ds(1, n_ch), :] + c0
    hid = hid * jax.nn.sigmoid(hid)
    return jnp.dot(hid.astype(BF16), w2_ref[j], preferred_element_type=F32)


def _compress_kernel(ksrc_ref, vsrc_ref, w1ab_ref, pe_ref, w1d_ref, b1_ref, w2_ref, ck_ref, cv_ref, pb_scr):
    n_ch = ksrc_ref.shape[0] // CMP_STRIDE

    def load(j, t, w_ref):
        x = (ksrc_ref, vsrc_ref)[j][pl.ds(t, n_ch, stride=CMP_STRIDE), :]
        return jnp.dot(x.astype(BF16), w_ref[j, t], preferred_element_type=F32)

    for j, out_ref in enumerate((ck_ref, cv_ref)):
        out_ref[0] = _compress_body(load, n_ch, w1ab_ref, pe_ref, w1d_ref, b1_ref, w2_ref, pb_scr, j).astype(BF16)


def _compress(cmp32, consts, batch, seq):
    n_ch = seq // CMP_STRIDE
    full = lambda a: pl.BlockSpec(a.shape, lambda b: (0,) * a.ndim)
    out = pl.BlockSpec((1, n_ch, KV_W), lambda b: (b, 0, 0))
    return pl.pallas_call(
        _compress_kernel,
        name="compress",
        grid=(batch,),
        in_specs=[pl.BlockSpec((seq, KV_W), lambda b: (b, 0)), pl.BlockSpec((seq, KV_W), lambda b: (b, 1))]
        + [full(a) for a in consts],
        out_specs=[out, out],
        out_shape=[jax.ShapeDtypeStruct((batch, n_ch, KV_W), BF16)] * 2,
        scratch_shapes=[pltpu.VMEM((n_ch + 8, KV_W), F32)],
        compiler_params=_cparams("parallel"),
    )(cmp32, cmp32, *consts)


def _cover(n_rows, n_cmp, n_sel, n_cols):
    c_start = np.arange(n_rows)[:, None] * CMP_STRIDE
    s_start = np.arange(n_cols)[None, :] * SEL_BLOCK
    ov = np.clip(np.minimum(c_start + CMP_BLOCK, s_start + SEL_BLOCK) - np.maximum(c_start, s_start), 0, None)
    ov = ov.astype(np.float32) / CMP_BLOCK
    ov[n_cmp:, :] = 0.0
    ov[:, n_sel:] = 0.0
    return jnp.asarray(ov, BF16)


def _kth_largest(count_ge, max_lt, lo, hi, n_valid, k):
    top = hi
    top_ok = count_ge(top) >= k

    def halve(_, c):
        lo, hi = c
        mid = 0.5 * lo + 0.5 * hi
        ge = count_ge(mid) >= k
        return jnp.where(ge, mid, lo), jnp.where(ge, hi, mid)

    def check(lo, hi):
        t = max_lt(hi)
        done = (count_ge(t) >= k) | top_ok | (n_valid < k)
        return t, jnp.where(done, 1.0, 0.0)

    def refine(c):
        it, lo, hi, _, _ = c
        lo, hi = lax.fori_loop(0, 8, halve, (lo, hi))
        return (it + 1, lo, hi) + check(lo, hi)

    lo, hi = lax.fori_loop(0, 30, halve, (lo, hi))
    state = (jnp.int32(0), lo, hi) + check(lo, hi)
    _, _, _, t, _ = lax.while_loop(lambda c: (c[0] < 40) & (jnp.min(c[4]) < 0.5), refine, state)
    return jnp.where(n_valid < k, -jnp.inf, jnp.where(top_ok, top, t))


def _split2(x):
    hi = x.astype(BF16)
    return hi, (x - hi.astype(F32)).astype(BF16)


def _nt(a, b):
    return lax.dot_general(a, b, (((1,), (1,)), ((), ())), preferred_element_type=F32)


def _group_queries(q_ref, g):
    return jnp.concatenate([q_ref[:, h * LANES:(h + 1) * LANES] for h in range(g * GROUP, (g + 1) * GROUP)], axis=0)


def _chain_init(st_ref, c):
    st_ref[c, 0] = jnp.full((GROUP * QB, LANES), NEG, F32)
    st_ref[c, 1] = jnp.zeros((GROUP * QB, LANES), F32)
    st_ref[c, 2] = jnp.zeros((GROUP * QB, LANES), F32)


def _chain_step(st_ref, c, q4, kblk, vblk, bias, ones):
    s = _nt(q4, kblk) + jnp.concatenate([bias] * GROUP, axis=0)
    m_old = st_ref[c, 0]
    m_new = jnp.maximum(m_old, jnp.max(s, axis=1, keepdims=True))
    alpha = jnp.exp(m_old - m_new)
    p = jnp.exp(s - m_new).astype(BF16)
    st_ref[c, 0] = m_new
    st_ref[c, 1] = alpha * st_ref[c, 1] + jnp.dot(p, ones, preferred_element_type=F32)
    st_ref[c, 2] = alpha * st_ref[c, 2] + jnp.dot(p, vblk, preferred_element_type=F32)


def _chain_result(st_ref, c, hh):
    rows = slice(hh * QB, (hh + 1) * QB)
    return st_ref[c, 2, rows, :] / st_ref[c, 1, rows, :]


def _pair_to_lanes(o_even, o_odd, h_even, lane):
    left = o_even if h_even // GROUP == 0 else pltpu.roll(o_even, HEAD_DIM, 1)
    right = o_odd if (h_even + 1) // GROUP == 1 else pltpu.roll(o_odd, HEAD_DIM, 1)
    return jnp.where(lane < HEAD_DIM, left, right)


def _topn_mask(v, n_cand, n_top, lane):
    rank = jnp.zeros(v.shape, F32)
    for i in range(n_cand):
        ci = v[:, i:i + 1]
        rank = rank + jnp.where((ci > v) | ((ci == v) & (lane > i)), 1.0, 0.0)
    return jnp.where((rank < n_top) & (lane < n_cand), 1.0, 0.0)


def _attn_prompt_kernel(qd_ref, qi_ref, qn_ref, iw_ref, nbg_ref,
                        kd_ref, vd_ref, ki_ref, ks_ref, vs_ref, kw_ref, vw_ref, ck_ref, cv_ref, cover_ref,
                        yb_ref, yc_ref, key_scr, db_scr, sb_scr, acc_scr, st_scr, *, n_top, n_sel):
    i = pl.program_id(1)
    nkb = i + 1
    row = lax.broadcasted_iota(I32, (QB, LANES), 0)
    lane = lax.broadcasted_iota(I32, (QB, LANES), 1)
    qpos = i * QB + row

    iw = iw_ref[...]

    def idx_blk(kb, c):
        kblk = ki_ref[pl.ds(pl.multiple_of(kb * QB, QB), QB), :]
        score = jnp.zeros((QB, LANES), F32)
        for h in range(IDX_HEADS):
            score = score + iw[:, h:h + 1] * jnp.maximum(_nt(qi_ref[:, h * LANES:(h + 1) * LANES], kblk), 0.0)
        key_scr[kb] = jnp.where(kb * QB + lane <= qpos, score, -jnp.inf)
        return c

    lax.fori_loop(0, nkb, idx_blk, 0)

    def sweep(fn, init, reduce):
        acc = lax.fori_loop(0, nkb, lambda kb, a: fn(a, key_scr[kb]), jnp.full((QB, LANES), init, F32))
        return reduce(acc, axis=1, keepdims=True)

    count_ge = lambda x: sweep(lambda a, k: a + jnp.where(k >= x, 1.0, 0.0), 0.0, jnp.sum)
    max_lt = lambda x: sweep(lambda a, k: jnp.maximum(a, jnp.where(k < x, k, -jnp.inf)), -jnp.inf, jnp.max)
    lo = sweep(lambda a, k: jnp.minimum(a, jnp.where(k == -jnp.inf, jnp.inf, k)), jnp.inf, jnp.min)
    hi = sweep(jnp.maximum, -jnp.inf, jnp.max)
    n_valid = (qpos[:, 0:1] + 1).astype(F32)
    thr = _kth_largest(count_ge, max_lt, lo, hi, n_valid, n_top)
    need = n_top - sweep(lambda a, k: a + jnp.where(k > thr, 1.0, 0.0), 0.0, jnp.sum)
    upper = jnp.where(row <= lane, 1.0, 0.0).astype(BF16)

    def mask_blk(kb, before):
        key = key_scr[kb]
        eq = key == thr
        pref = jnp.dot(jnp.where(eq, 1.0, 0.0).astype(BF16), upper, preferred_element_type=F32) + before
        sel = ((key > thr) | (eq & (pref <= need))) & (kb * QB + lane <= qpos)
        db_scr[kb] = jnp.where(sel, 0.0, NEG)
        return before + jnp.sum(jnp.where(eq, 1.0, 0.0), axis=1, keepdims=True)

    lax.fori_loop(0, nkb, mask_blk, jnp.zeros((QB, 1), F32))

    gb = jax.nn.sigmoid(nbg_ref[...])
    gcol = lambda h, r: gb[:, 3 * h + r:3 * h + r + 1]
    ck, cv = ck_ref[0], cv_ref[0]
    n_ch = ck.shape[0]
    cl = lax.broadcasted_iota(I32, (QB, n_ch), 1)
    cvalid = cl * CMP_STRIDE + (CMP_BLOCK - 1) <= i * QB + lax.broadcasted_iota(I32, (QB, n_ch), 0)
    for g in range(KV_HEADS):
        imp = jnp.zeros((QB, n_ch), F32)
        for h in range(g * GROUP, (g + 1) * GROUP):
            s = jnp.where(cvalid, _nt(qn_ref[:, h * LANES:(h + 1) * LANES], ck), NEG)
            e = jnp.exp(s - jnp.max(s, axis=1, keepdims=True))
            p = jnp.where(cvalid, e / jnp.sum(e, axis=1, keepdims=True), 0.0)
            acc_scr[h] = gcol(h, 0) * jnp.dot(p.astype(BF16), cv, preferred_element_type=F32)
            imp = imp + p
        ih, il = _split2(imp)
        impj = (jnp.dot(ih, cover_ref[...], preferred_element_type=F32)
                + jnp.dot(il, cover_ref[...], preferred_element_type=F32))
        tb = qpos >> 6
        forced = (lane == 0) | (lane == tb) | (lane == tb - 1)
        v = jnp.where(lane * SEL_BLOCK <= qpos, jnp.where(forced, FORCE, impj), NEG)
        sel16 = _topn_mask(v, n_sel, min(SEL_TOPN, n_sel), lane).astype(BF16)

        def sb_blk(kb, c, g=g, sel16=sel16):
            expand = jnp.where(row == 2 * kb + (lane >> 6), 1.0, 0.0).astype(BF16)
            tok = jnp.dot(sel16, expand, preferred_element_type=F32)
            sb_scr[g, kb] = jnp.where((tok > 0.5) & (kb * QB + lane <= qpos), 0.0, NEG)
            return c

        lax.fori_loop(0, nkb, sb_blk, 0)

    ones = jnp.ones((LANES, LANES), BF16)
    for c in range(3 * KV_HEADS):
        _chain_init(st_scr, c)

    def blk_dsa_sel(kb, c):
        r = pl.ds(pl.multiple_of(kb * QB, QB), QB)
        for g in range(KV_HEADS):
            _chain_step(st_scr, g, _group_queries(qd_ref, g), kd_ref[r, :], vd_ref[r, :], db_scr[kb], ones)
            _chain_step(st_scr, KV_HEADS + g, _group_queries(qn_ref, g), ks_ref[r, :], vs_ref[r, :], sb_scr[g, kb],
                        ones)
        return c

    def blk_win(kb, c):
        r = pl.ds(pl.multiple_of(kb * QB, QB), QB)
        kpos = kb * QB + lane
        bias = jnp.where((kpos <= qpos) & (qpos - kpos < WINDOW), 0.0, NEG)
        for g in range(KV_HEADS):
            _chain_step(st_scr, 2 * KV_HEADS + g, _group_queries(qn_ref, g), kw_ref[r, :], vw_ref[r, :], bias, ones)
        return c

    lax.fori_loop(0, nkb, blk_dsa_sel, 0)
    lax.fori_loop(jnp.maximum(i - WINDOW // QB, 0), nkb, blk_win, 0)

    for j in range(N_HEADS // 2):
        cols = slice(j * LANES, (j + 1) * LANES)
        o_d, o_c = [], []
        for h in (2 * j, 2 * j + 1):
            g, hh = divmod(h, GROUP)
            o_d.append(_chain_result(st_scr, g, hh))
            o_c.append(acc_scr[h] + gcol(h, 1) * _chain_result(st_scr, KV_HEADS + g, hh)
                       + gcol(h, 2) * _chain_result(st_scr, 2 * KV_HEADS + g, hh))
        yb_ref[:, cols] = _pair_to_lanes(o_d[0], o_d[1], 2 * j, lane)
        yc_ref[:, cols] = _pair_to_lanes(o_c[0], o_c[1], 2 * j, lane)


def _attn_prompt(qd, qi, qn, iw, z, kv16, ck, cv, batch, seq):
    nq = seq // QB
    n_ch = seq // CMP_STRIDE
    n_sel = seq // SEL_BLOCK
    n_cmp = (seq - CMP_BLOCK) // CMP_STRIDE + 1
    cover = _cover(n_ch, n_cmp, n_sel, LANES)
    qspec = pl.BlockSpec((QB, N_HEADS * LANES), lambda b, i: (b * nq + i, 0))
    zspec = lambda name, w: pl.BlockSpec((QB, w), lambda b, i, o=Z_OFF[name] // w: (b * nq + i, o))
    kvspec = lambda blk: pl.BlockSpec((seq, LANES), lambda b, i: (b, blk))
    cspec = pl.BlockSpec((1, n_ch, KV_W), lambda b, i: (b, 0, 0))
    yspec = pl.BlockSpec((QB, BRANCH_W), lambda b, i: (b * nq + i, 0))
    kern = functools.partial(_attn_prompt_kernel, n_top=min(DSA_TOPK, seq // 4), n_sel=n_sel)
    return pl.pallas_call(
        kern,
        name="attn_prompt",
        grid=(batch, nq),
        in_specs=[qspec, qspec, qspec, pl.BlockSpec((QB, LANES), lambda b, i: (b * nq + i, 0)),
                  zspec("nbg", LANES),
                  kvspec(KV_KD), kvspec(KV_VD), kvspec(KV_KI), kvspec(KV_KS), kvspec(KV_VS), kvspec(KV_KW),
                  kvspec(KV_VW), cspec, cspec, pl.BlockSpec(cover.shape, lambda b, i: (0, 0))],
        out_specs=[yspec, yspec],
        out_shape=[jax.ShapeDtypeStruct((batch * seq, BRANCH_W), F32)] * 2,
        scratch_shapes=[pltpu.VMEM((nq, QB, LANES), F32), pltpu.VMEM((nq, QB, LANES), F32),
                        pltpu.VMEM((KV_HEADS, nq, QB, LANES), F32), pltpu.VMEM((N_HEADS, QB, LANES), F32),
                        pltpu.VMEM((3 * KV_HEADS, 3, GROUP * QB, LANES), F32)],
        compiler_params=_cparams("parallel", "arbitrary"),
    )(qd, qi, qn, iw, z, kv16, kv16, kv16, kv16, kv16, kv16, kv16, ck, cv, cover)


C_DK, C_DV, C_IK, C_CKV, C_CVS, C_SKV = range(6)
N_CBLK = 6


def _key_chunk(n_blocks):
    return QB * max(d for d in range(1, 17) if n_blocks % d == 0)


def _attn_sample_kernel(pt_ref, cache_ref, new_ref, qd_ref, qi_ref, qn_ref, iw_ref, nbg_ref, win_ref,
                        w1s_ref, pe_ref, w1d_ref, b1_ref, w2_ref, cover_ref,
                        od_ref, oc_ref, buf, tail, pb_scr, tok_scr, pref_scr, sem, *, layer, past, n_top):
    b = pl.program_id(0)
    n_pages = past // PAGE_SIZE
    nk = past + QB
    kc = _key_chunk(nk // QB)
    n_ch = past // CMP_STRIDE
    n_sel = -(-(past + 1) // SEL_BLOCK)
    n_selp = cover_ref.shape[1]

    def page_copies(p):
        src = cache_ref.at[layer, pt_ref[b, p]]
        rows = pl.ds(pl.multiple_of(p * PAGE_SIZE, PAGE_SIZE), PAGE_SIZE)
        cps = [pltpu.make_async_copy(src.at[:, pl.ds(j * LANES, LANES)], buf.at[j, rows, :], sem)
               for j in range(N_CBLK)]
        return cps + [pltpu.make_async_copy(src.at[:, pl.ds(N_CBLK * LANES, HEAD_DIM)], tail.at[rows, :], sem)]

    def start(p, c):
        for cp in page_copies(p):
            cp.start()
        return c

    def wait(p, c):
        for cp in page_copies(p):
            cp.wait()
        return c

    lax.fori_loop(0, n_pages, start, 0)
    new = new_ref[0]
    for j in range(N_CBLK):
        buf[j, pl.ds(past, QB), :] = jnp.zeros((QB, LANES), F32)
        buf[j, pl.ds(past, 1), :] = new[:, j * LANES:(j + 1) * LANES]
    tail[pl.ds(past, QB), :] = jnp.zeros((QB, HEAD_DIM), F32)
    tail[pl.ds(past, 1), :] = new[:, N_CBLK * LANES:]
    lax.fori_loop(0, n_pages, wait, 0)

    row8 = lax.broadcasted_iota(I32, (N_HEADS, nk), 0)
    kpos1 = lax.broadcasted_iota(I32, (1, nk), 1)
    in_range = kpos1 <= past
    lo_heads = lax.broadcasted_iota(I32, (N_HEADS, HEAD_DIM), 0) < GROUP

    def scores(q, blk):
        return jnp.concatenate([_nt(q, buf[blk, c:c + kc, :]) for c in range(0, nk, kc)], axis=1)

    def weighted(p, blk):
        vals = (lambda c: tail[c:c + kc, :]) if blk is None else (lambda c: buf[blk, c:c + kc, :])
        return sum(jnp.dot(p[:, c:c + kc], vals(c), preferred_element_type=F32) for c in range(0, nk, kc))

    def softmax(s):
        e = jnp.exp(s - jnp.max(s, axis=1, keepdims=True))
        return e / jnp.sum(e, axis=1, keepdims=True)

    compact = lambda o: jnp.where(lo_heads, o[:, :HEAD_DIM], o[:, HEAD_DIM:])

    sc = jnp.sum(iw_ref[0] * jnp.maximum(scores(qi_ref[0], C_IK), 0.0), axis=0, keepdims=True)
    key = jnp.where(in_range, sc, -jnp.inf)
    count = lambda pred: jnp.sum(jnp.where(pred, 1.0, 0.0), axis=1, keepdims=True)
    thr = _kth_largest(lambda x: count(key >= x),
                       lambda x: jnp.max(jnp.where(key < x, key, -jnp.inf), axis=1, keepdims=True),
                       jnp.min(jnp.where(in_range, sc, jnp.inf), axis=1, keepdims=True),
                       jnp.max(key, axis=1, keepdims=True), jnp.full((1, 1), past + 1.0, F32), n_top)
    need = n_top - count(key > thr)
    eq8 = jnp.where(jnp.broadcast_to(key == thr, (N_HEADS, nk)), 1.0, 0.0)
    upper = jnp.where(lax.broadcasted_iota(I32, (LANES, LANES), 0) <= lax.broadcasted_iota(I32, (LANES, LANES), 1),
                      1.0, 0.0).astype(BF16)
    before = jnp.zeros((N_HEADS, 1), F32)
    for c in range(0, nk, LANES):
        pref = jnp.dot(eq8[:, c:c + LANES].astype(BF16), upper, preferred_element_type=F32) + before
        pref_scr[:, c:c + LANES] = pref
        before = before + jnp.sum(eq8[:, c:c + LANES], axis=1, keepdims=True)
    sel = ((key > thr) | ((key == thr) & (pref_scr[0:1, :] <= need))) & in_range
    p = softmax(scores(qd_ref[0], C_DK) + jnp.where(sel, 0.0, NEG))
    od_ref[0] = compact(weighted(p, C_DV))

    def load(j, t, w_ref):
        blks = (C_IK, C_CKV) if j == 0 else (C_CKV, C_CVS)
        return sum(jnp.dot(buf[blk, pl.ds(t, n_ch, stride=CMP_STRIDE), :].astype(BF16), w_ref[j, i, t],
                           preferred_element_type=F32) for i, blk in enumerate(blks))

    ck = _compress_body(load, n_ch, w1s_ref, pe_ref, w1d_ref, b1_ref, w2_ref, pb_scr, 0).astype(BF16)
    cv = _compress_body(load, n_ch, w1s_ref, pe_ref, w1d_ref, b1_ref, w2_ref, pb_scr, 1).astype(BF16)

    gb = jax.nn.sigmoid(nbg_ref[0])
    qn = qn_ref[0]
    cvalid = lax.broadcasted_iota(I32, (N_HEADS, n_ch), 1) * CMP_STRIDE + (CMP_BLOCK - 1) <= past
    s = jnp.where(cvalid, _nt(qn.astype(BF16), ck), NEG)
    e = jnp.exp(s - jnp.max(s, axis=1, keepdims=True))
    pc = jnp.where(cvalid, e / jnp.sum(e, axis=1, keepdims=True), 0.0)
    o_cmp = compact(jnp.dot(pc.astype(BF16), cv, preferred_element_type=F32))
    imp = jnp.where(lax.broadcasted_iota(I32, (N_HEADS, n_ch), 0) < GROUP,
                    jnp.sum(pc[:GROUP], axis=0, keepdims=True), jnp.sum(pc[GROUP:], axis=0, keepdims=True))
    ih, il = _split2(imp)
    impj = (jnp.dot(ih, cover_ref[...], preferred_element_type=F32)
            + jnp.dot(il, cover_ref[...], preferred_element_type=F32))
    lane_s = lax.broadcasted_iota(I32, (N_HEADS, n_selp), 1)
    tb = past // SEL_BLOCK
    forced = (lane_s == 0) | (lane_s == tb) | (lane_s == tb - 1)
    v = jnp.where(lane_s * SEL_BLOCK <= past, jnp.where(forced, FORCE, impj), NEG)
    sel8 = _topn_mask(v, n_sel, min(SEL_TOPN, n_sel), lane_s)
    half = lax.broadcasted_iota(I32, (N_HEADS, LANES), 1) < SEL_BLOCK
    for c in range(nk // LANES):
        tok_scr[:, c * LANES:(c + 1) * LANES] = jnp.where(half, sel8[:, 2 * c:2 * c + 1], sel8[:, 2 * c + 1:2 * c + 2])

    qsw = pltpu.roll(qn, HEAD_DIM, 1)
    s = jnp.where(row8 < GROUP, scores(qsw, C_CVS), scores(qsw, C_SKV))
    p = softmax(jnp.where((tok_scr[...] > 0.5) & in_range, s, NEG))
    o_sel = jnp.where(lo_heads, weighted(p, C_SKV)[:, HEAD_DIM:], weighted(p, None))

    pw = softmax(_nt(qn, win_ref[0, :, 0:KV_W]))
    o_win = compact(jnp.dot(pw, win_ref[0, :, KV_W:2 * KV_W], preferred_element_type=F32))
    oc_ref[0] = gb[:, 0:1] * o_cmp + gb[:, 1:2] * o_sel + gb[:, 2:3] * o_win


def _compress_consts_sample(w1ab):
    sw = jnp.concatenate([w1ab[:, :, HEAD_DIM:], w1ab[:, :, :HEAD_DIM]], axis=2)
    upper = jnp.concatenate([jnp.zeros_like(sw[:, :, :HEAD_DIM]), sw[:, :, HEAD_DIM:]], axis=2)
    lower = jnp.concatenate([sw[:, :, :HEAD_DIM], jnp.zeros_like(sw[:, :, HEAD_DIM:])], axis=2)
    return jnp.stack([upper, lower], axis=1)


def _attn_sample(page_table, cache_kv, layer, new_rows, qd, qi, qn, iw, nbg, win, cmp_consts):
    nb, n_pages = page_table.shape
    past = n_pages * PAGE_SIZE
    nk = past + QB
    n_ch = past // CMP_STRIDE
    n_sel = -(-(past + 1) // SEL_BLOCK)
    n_selp = -(-n_sel // LANES) * LANES
    n_cmp = (past + 1 - CMP_BLOCK) // CMP_STRIDE + 1
    cover = _cover(n_ch, n_cmp, n_sel, n_selp)
    w1ab, pe8, w1d, b1d, w2bd = cmp_consts
    w1s = _compress_consts_sample(w1ab)
    consts = (w1s, pe8, w1d, b1d, w2bd, cover)
    per_seq = lambda a: pl.BlockSpec((1,) + a.shape[1:], lambda b, pt: (b,) + (0,) * (a.ndim - 1))
    full = lambda a: pl.BlockSpec(a.shape, lambda b, pt: (0,) * a.ndim)
    args = (new_rows, qd, qi, qn, iw, nbg, win)
    kern = functools.partial(_attn_sample_kernel, layer=layer, past=past, n_top=min(DSA_TOPK, (past + 1) // 4))
    out = pl.BlockSpec((1, N_HEADS, HEAD_DIM), lambda b, pt: (b, 0, 0))
    return pl.pallas_call(
        kern,
        name="attn_sample",
        grid_spec=pltpu.PrefetchScalarGridSpec(
            num_scalar_prefetch=1,
            grid=(nb,),
            in_specs=[pl.BlockSpec(memory_space=pl.ANY)] + [per_seq(a) for a in args] + [full(a) for a in consts],
            out_specs=[out, out],
            scratch_shapes=[pltpu.VMEM((N_CBLK, nk, LANES), F32), pltpu.VMEM((nk, HEAD_DIM), F32),
                            pltpu.VMEM((n_ch + 8, KV_W), F32),
                            pltpu.VMEM((N_HEADS, nk), F32), pltpu.VMEM((N_HEADS, nk), F32),
                            pltpu.SemaphoreType.DMA(())]),
        out_shape=[jax.ShapeDtypeStruct((nb, N_HEADS, HEAD_DIM), F32)] * 2,
        compiler_params=_cparams("arbitrary"),
    )(page_table, cache_kv, *args, *consts)


def _merge_kernel(x_ref, ya_ref, yb_ref, yc_ref, dg_ref, ng_ref, ga_ref, gb_ref, gc_ref, wb_ref, wo_ref, y_ref):
    silu = lambda v: v * jax.nn.sigmoid(v)
    ys = (ya_ref[...], yb_ref[...] * silu(dg_ref[...]), yc_ref[...] * silu(ng_ref[...]))
    m = None
    for n, (y, g) in enumerate(zip(ys, (ga_ref, gb_ref, gc_ref))):
        pr = jax.nn.sigmoid(g[...]) * jnp.dot(y.astype(BF16), wb_ref[n], preferred_element_type=F32)
        m = pr if m is None else m + pr
    y_ref[...] = x_ref[...] + jnp.dot(m.astype(BF16), wo_ref[...], preferred_element_type=F32)


def _merge(x2, ya, yb, yc, z, wb16, wo16):
    t, d = x2.shape
    tm = min(t, 512)
    yspec = pl.BlockSpec((tm, BRANCH_W), lambda i: (i, 0))
    gspec = lambda name, w=d: pl.BlockSpec((tm, w), lambda i, o=Z_OFF[name] // w: (i, o))
    return pl.pallas_call(
        _merge_kernel,
        name="merge",
        grid=(t // tm,),
        in_specs=[pl.BlockSpec((tm, d), lambda i: (i, 0)), yspec, yspec, yspec,
                  gspec("dgate", BRANCH_W), gspec("ngate", BRANCH_W), gspec("ma"), gspec("mb"), gspec("mc"),
                  pl.BlockSpec(wb16.shape, lambda i: (0, 0, 0)), pl.BlockSpec(wo16.shape, lambda i: (0, 0))],
        out_specs=pl.BlockSpec((tm, d), lambda i: (i, 0)),
        out_shape=jax.ShapeDtypeStruct((t, d), F32),
        compiler_params=_cparams("parallel"),
    )(x2, ya, yb, yc, z, z, z, z, z, wb16, wo16)


def _layer_prompt(x, lb, norm_g, w16, gon, prep_consts, cmp_consts, wb16, wo16):
    batch, seq, d = x.shape
    x2 = x.reshape(batch * seq, d)
    z = _proj(x2, norm_g, w16)
    tabs = _rope_tables(jnp.arange(seq, dtype=I32))
    rows, wrows, cmp32, qd, qi, qn, iw, kv16 = _prep(z, tabs, *prep_consts, min(seq, 256))
    ya, s_new = _hgrn_prompt(z, lb, gon, batch, seq)
    ck, cv = _compress(cmp32, cmp_consts, batch, seq)
    yb, yc = _attn_prompt(qd, qi, qn, iw, z, kv16, ck, cv, batch, seq)
    y = _merge(x2, ya, yb, yc, z, wb16, wo16).reshape(batch, seq, d)
    wrows = wrows.reshape(batch, seq, WIN_W)
    return y, rows.reshape(batch, seq, CACHE_W), wrows[:, -min(WINDOW, seq):], s_new


def _layer_sample(x, lb, norm_g, w16, gon, prep_consts, cmp_consts, wb16, wo16,
                  page_table, cache_kv, layer, win_rows, s0):
    nb, one, d = x.shape
    assert one == 1 and win_rows.shape[1] == WINDOW
    past = page_table.shape[1] * PAGE_SIZE
    x2 = x.reshape(nb, d)
    z = _proj(x2, norm_g, w16)
    tabs = _rope_tables(jnp.full((nb,), past, I32))
    rows, wrows, _, qd, qi, qn, iw, _ = _prep(z, tabs, *prep_consts, nb)
    ya, s_new = _hgrn_sample(z, lb, gon, s0)
    win = jnp.concatenate([win_rows[:, 1:], wrows[:, None, :]], axis=1)
    heads = lambda q: q.astype(F32).reshape(nb, N_HEADS, LANES)
    nbg = z[:, Z_OFF["nbg"]:Z_OFF["nbg"] + 3 * N_HEADS].reshape(nb, N_HEADS, 3)
    od, oc = _attn_sample(page_table, cache_kv, layer, rows.reshape(nb, 1, CACHE_W), heads(qd), heads(qi), heads(qn),
                          iw[:, :IDX_HEADS].reshape(nb, IDX_HEADS, 1), nbg, win, cmp_consts)
    y = _merge(x2, ya, od.reshape(nb, BRANCH_W), oc.reshape(nb, BRANCH_W), z, wb16, wo16)
    return y.reshape(nb, 1, d), rows.reshape(nb, 1, CACHE_W), win, s_new


def kernel(x_prompt, x_sample, cache_kv, page_table, state_win, state_hgrn, norm_g, w_in,
           hgrn_lb_logits, hgrn_onorm_g, dsa_qk_norm_g, dsa_idx_k_norm_g, nsa_qk_norm_g,
           nsa_cmp_pe, nsa_cmp_w1, nsa_cmp_b1, nsa_cmp_w2, w_branch, w_out):
    lb_soft = jax.nn.softmax(hgrn_lb_logits.astype(F32), axis=0)
    lbs = jnp.cumsum(lb_soft, axis=0) - lb_soft[0]
    xp, xs = x_prompt, x_sample
    outs = [[] for _ in range(6)]
    for l in range(w_in.shape[0]):
        shared = (lbs[l], norm_g[l], _perm_w_in(w_in[l]), hgrn_onorm_g[l],
                  _prep_consts(dsa_qk_norm_g[l], dsa_idx_k_norm_g[l], nsa_qk_norm_g[l]),
                  _compress_consts(nsa_cmp_pe[l], nsa_cmp_w1[l], nsa_cmp_b1[l], nsa_cmp_w2[l]),
                  w_branch[l].astype(BF16), w_out[l].astype(BF16))
        xs, *s_out = _layer_sample(xs, *shared, page_table, cache_kv, l, state_win[l], state_hgrn[l])
        xp, *p_out = _layer_prompt(xp, *shared)
        for acc, o in zip(outs, p_out + s_out):
            acc.append(o)
    return (xp, xs) + tuple(jnp.stack(o) for o in outs)
```

```python
import functools

import jax
import jax.numpy as jnp
from jax import lax
import numpy as np
from jax.experimental import pallas as pl
from jax.experimental.pallas import tpu as pltpu

F32 = jnp.float32
BF16 = jnp.bfloat16
I32 = jnp.int32

D_MODEL = 1024
PAGE_SIZE = 128
BRANCH_W = D_MODEL // 2
HEAD_DIM = 64
ROPE_THETA = 500000.0
EPS = 1e-6
NEG = -1e30
FORCE = 1e9
HG_HEADS = 4
HG_K = BRANCH_W // HG_HEADS
HG_V = BRANCH_W // HG_HEADS
HG_CHUNK = 64
N_HEADS = BRANCH_W // HEAD_DIM
KV_HEADS = 2
GROUP = N_HEADS // KV_HEADS
IDX_HEADS = 8
IDX_DIM = 64
DSA_TOPK = 256
CMP_BLOCK = 32
CMP_STRIDE = 16
SEL_BLOCK = 64
SEL_TOPN = 16
WINDOW = 512
KV_W = KV_HEADS * HEAD_DIM
CACHE_W = 2 * KV_W + IDX_DIM + 4 * KV_W
WIN_W = 2 * KV_W
LANES = 128
QB = 128
INT_MIN = np.int32(-2 ** 31)
VMEM_LIMIT = 56 * 1024 * 1024

IN_SPLITS = (
    ("hq", 512), ("hf", 512), ("hi", 512), ("hgate", 512),
    ("dq", 512), ("dk", 128), ("dv", 128), ("iq", 512), ("ik", 64), ("iw", 8), ("dgate", 512),
    ("nq", 512), ("nck", 128), ("ncv", 128), ("nsk", 128), ("nsv", 128), ("nwk", 128), ("nwv", 128),
    ("nbg", 24), ("ngate", 512), ("ma", 1024), ("mb", 1024), ("mc", 1024),
)
Z_ORDER = ("ma", "mb", "mc", "hq", "hf", "hi", "hgate", "dq", "iq", "nq", "dgate", "ngate",
           "dk", "dv", "ik", "iw", "nck", "ncv", "nsk", "nsv", "nwk", "nwv", "nbg")
Z_COLS = 9216


def _z_layout():
    widths = dict(IN_SPLITS)
    src, off = {}, 0
    for name, w in IN_SPLITS:
        src[name] = off
        off += w
    dst, off = {}, 0
    for name in Z_ORDER:
        dst[name] = off
        off += -(-widths[name] // LANES) * LANES
    assert off <= Z_COLS
    idx = np.zeros((Z_COLS,), np.int32)
    valid = np.zeros((Z_COLS,), np.float32)
    for name in Z_ORDER:
        w = widths[name]
        idx[dst[name]:dst[name] + w] = src[name] + np.arange(w)
        valid[dst[name]:dst[name] + w] = 1.0
    return dst, idx, valid


Z_OFF, Z_SRC_IDX, Z_VALID = _z_layout()


def _cparams(*sem):
    return pltpu.CompilerParams(dimension_semantics=sem, vmem_limit_bytes=VMEM_LIMIT)


def _proj_kernel(x_ref, g_ref, w_ref, z_ref, h_scr):
    @pl.when(pl.program_id(1) == 0)
    def _():
        x = x_ref[...]
        ms = jnp.mean(x * x, axis=-1, keepdims=True)
        h_scr[...] = (x * lax.rsqrt(ms + EPS) * g_ref[...]).astype(BF16)

    z_ref[...] = jnp.dot(h_scr[...], w_ref[...], preferred_element_type=F32)


def _proj(x2, g, w16):
    t, d = x2.shape
    tm = min(t, 512)
    tn = 1536
    return pl.pallas_call(
        _proj_kernel,
        name="proj",
        grid=(t // tm, Z_COLS // tn),
        in_specs=[pl.BlockSpec((tm, d), lambda i, j: (i, 0)),
                  pl.BlockSpec((1, d), lambda i, j: (0, 0)),
                  pl.BlockSpec((d, tn), lambda i, j: (0, j))],
        out_specs=pl.BlockSpec((tm, tn), lambda i, j: (i, j)),
        out_shape=jax.ShapeDtypeStruct((t, Z_COLS), F32),
        scratch_shapes=[pltpu.VMEM((tm, d), BF16)],
        compiler_params=_cparams("parallel", "arbitrary"),
    )(x2, g.reshape(1, d), w16)


KV_KD, KV_VD, KV_KI, KV_KS, KV_VS, KV_KW, KV_VW = range(7)
G_DQ, G_DK, G_IK, G_NQ, G_NCK, G_NSK, G_NWK = range(7)


def _head_rms(x, gain, bd):
    x2 = x * x
    hi = x2.astype(BF16)
    lo = (x2 - hi.astype(F32)).astype(BF16)
    ms = jnp.dot(hi, bd, preferred_element_type=F32) + jnp.dot(lo, bd, preferred_element_type=F32)
    return x * lax.rsqrt(ms + EPS) * gain


def _rope128(x, cosf, sina, sinb):
    return x * cosf + pltpu.roll(x, LANES - 8, 1) * sina + pltpu.roll(x, 8, 1) * sinb


def _prep_kernel(dq_ref, iq_ref, nq_ref, s0_ref, s1_ref, s2_ref, cos_ref, sina_ref, sinb_ref, g_ref, bd_ref,
                 rows_ref, wrows_ref, cmp_ref, qd_ref, qi_ref, qn_ref, iw_ref, kv_ref):
    cosf, sina, sinb = cos_ref[...], sina_ref[...], sinb_ref[...]
    bd = bd_ref[...]
    lane = lax.broadcasted_iota(I32, cosf.shape, 1)
    lo_half = lane < HEAD_DIM
    rope = lambda v: _rope128(v, cosf, sina, sinb)
    gain = lambda r: g_ref[r:r + 1, :]

    def put_q(src_ref, out_ref, g_row, scale, target_of_head):
        for j in range(N_HEADS // 2):
            x = src_ref[:, j * LANES:(j + 1) * LANES]
            if g_row is not None:
                x = _head_rms(x, gain(g_row), bd)
            x = rope(x) * scale
            xr = pltpu.roll(x, HEAD_DIM, 1)
            for half in range(2):
                h = 2 * j + half
                tgt = target_of_head(h)
                v = x if tgt == half else xr
                keep = lo_half if tgt == 0 else jnp.logical_not(lo_half)
                out_ref[:, h * LANES:(h + 1) * LANES] = jnp.where(keep, v, 0.0).astype(BF16)

    put_q(dq_ref, qd_ref, G_DQ, HEAD_DIM ** -0.5, lambda h: h // GROUP)
    put_q(iq_ref, qi_ref, None, 1.0, lambda h: 0)
    put_q(nq_ref, qn_ref, G_NQ, HEAD_DIM ** -0.5, lambda h: h // GROUP)

    dk = rope(_head_rms(s0_ref[:, 0:128], gain(G_DK), bd))
    dv = s0_ref[:, 128:256]
    ik = rope(_head_rms(s0_ref[:, 256:384], gain(G_IK), bd))
    iw_ref[...] = s0_ref[:, 384:512] * (IDX_HEADS ** -0.5 * IDX_DIM ** -0.5)
    nck = rope(_head_rms(s1_ref[:, 0:128], gain(G_NCK), bd))
    ncv = s1_ref[:, 128:256]
    nsk = rope(_head_rms(s1_ref[:, 256:384], gain(G_NSK), bd))
    nsv = s1_ref[:, 384:512]
    nwk = rope(_head_rms(s2_ref[:, 0:128], gain(G_NWK), bd))
    nwv = s2_ref[:, 128:256]

    rows_ref[:, 0:128] = dk
    rows_ref[:, 128:256] = dv
    rows_ref[:, 256:320] = ik[:, 0:64]
    rows_ref[:, 320:448] = nck
    rows_ref[:, 448:576] = ncv
    rows_ref[:, 576:704] = nsk
    rows_ref[:, 704:832] = nsv
    wrows_ref[:, 0:128] = nwk
    wrows_ref[:, 128:256] = nwv
    cmp_ref[:, 0:128] = nck
    cmp_ref[:, 128:256] = ncv
    for blk, v in ((KV_KD, dk), (KV_VD, dv), (KV_KI, ik), (KV_KS, nsk), (KV_VS, nsv), (KV_KW, nwk), (KV_VW, nwv)):
        kv_ref[:, blk * LANES:(blk + 1) * LANES] = v.astype(BF16)


def _prep(z, tabs, gains, bd, tm):
    t = z.shape[0]
    nt = tabs[0].shape[0] // tm
    zb = lambda blk: pl.BlockSpec((tm, 512), lambda i: (i, blk))
    tab = pl.BlockSpec((tm, LANES), lambda i: (i % nt, 0))
    full = lambda a: pl.BlockSpec(a.shape, lambda i: (0,) * a.ndim)
    outs = [(CACHE_W, F32), (WIN_W, F32), (2 * KV_W, F32), (N_HEADS * LANES, BF16), (N_HEADS * LANES, BF16),
            (N_HEADS * LANES, BF16), (LANES, F32), (7 * LANES, BF16)]
    return pl.pallas_call(
        _prep_kernel,
        name="prep",
        grid=(t // tm,),
        in_specs=[zb(Z_OFF["dq"] // 512), zb(Z_OFF["iq"] // 512), zb(Z_OFF["nq"] // 512),
                  zb(Z_OFF["dk"] // 512), zb(Z_OFF["nck"] // 512), zb(Z_OFF["nwk"] // 512),
                  tab, tab, tab, full(gains), full(bd)],
        out_specs=[pl.BlockSpec((tm, w), lambda i: (i, 0)) for w, _ in outs],
        out_shape=[jax.ShapeDtypeStruct((t, w), dt) for w, dt in outs],
        compiler_params=_cparams("parallel"),
    )(z, z, z, z, z, z, *tabs, gains, bd)


def _rope_tables(pos):
    half = HEAD_DIM // 8
    inv = ROPE_THETA ** (-jnp.arange(half, dtype=F32) / half)
    ang = pos.astype(F32)[:, None] * inv[None, :]
    cos, sin = jnp.cos(ang), jnp.sin(ang)
    n = pos.shape[0]
    one = jnp.ones((n, HEAD_DIM - 2 * half), F32)
    zero = jnp.zeros((n, HEAD_DIM - 2 * half), F32)
    z8 = jnp.zeros((n, half), F32)
    cosf = jnp.concatenate([cos, cos, one], axis=1)
    sina = jnp.concatenate([-sin, z8, zero], axis=1)
    sinb = jnp.concatenate([z8, sin, zero], axis=1)
    return tuple(jnp.tile(a, (1, 2)) for a in (cosf, sina, sinb))


def _perm_w_in(w_in):
    return (jnp.take(w_in, jnp.asarray(Z_SRC_IDX), axis=1) * jnp.asarray(Z_VALID)[None, :]).astype(BF16)


def _prep_consts(dsa_qk_g, idx_k_g, nsa_qk_g):
    two = lambda g: jnp.tile(g.astype(F32), 2)
    ikg = jnp.concatenate([idx_k_g.astype(F32), jnp.ones((HEAD_DIM,), F32)])
    rows = [two(dsa_qk_g[0]), two(dsa_qk_g[1]), ikg, two(nsa_qk_g[0]), two(nsa_qk_g[1]), two(nsa_qk_g[2]),
            two(nsa_qk_g[3]), jnp.ones((LANES,), F32)]
    bd = np.kron(np.eye(2, dtype=np.float32), np.full((HEAD_DIM, HEAD_DIM), 1.0 / HEAD_DIM, np.float32))
    return jnp.stack(rows), jnp.asarray(bd, BF16)


def _split3(x):
    a = x.astype(BF16)
    r = x - a.astype(F32)
    b = r.astype(BF16)
    c = (r - b.astype(F32)).astype(BF16)
    return a, b, c


def _row_to_col(row, eye):
    return jnp.sum(jnp.where(eye, row, 0.0), axis=1, keepdims=True)


def _hgrn_gates(fr, lb):
    sig = jax.nn.sigmoid(fr)
    f = lb + (1.0 - lb) * sig
    return jnp.log(f), (1.0 - lb) * jax.nn.sigmoid(-fr)


def _hgrn_out(o, gon, gate):
    ms = jnp.mean(o * o, axis=-1, keepdims=True)
    return o * lax.rsqrt(ms + EPS) * gon * (gate * jax.nn.sigmoid(gate))


def _hgrn_prompt_kernel(hq_ref, hf_ref, hi_ref, hg_ref, lb_ref, gon_ref, ya_ref, s_ref, b_scr, k_scr):
    c = HG_CHUNK
    n_chunks = hq_ref.shape[0] // c
    lb, gon = lb_ref[...], gon_ref[...]
    ti = lax.broadcasted_iota(I32, (c, HG_K), 0)
    tril = (lax.broadcasted_iota(I32, (c, c), 0) >= lax.broadcasted_iota(I32, (c, c), 1)).astype(BF16)
    lane_c = lax.broadcasted_iota(I32, (c, c), 1)
    eye = lax.broadcasted_iota(I32, (HG_K, HG_K), 0) == lax.broadcasted_iota(I32, (HG_K, HG_K), 1)
    s_ref[0, 0] = jnp.zeros((HG_K, HG_V), F32)

    def chunk(ci, carry):
        rows = pl.ds(pl.multiple_of(ci * c, c), c)
        g, kk = _hgrn_gates(hf_ref[rows, :], lb)
        hq = hq_ref[rows, :]
        qq = hq * jax.nn.sigmoid(hq)
        v = hi_ref[rows, :]
        b = sum(jnp.dot(tril, part, preferred_element_type=F32) for part in _split3(g))
        b_scr[...] = b
        k_scr[...] = kk
        a_tiles = []
        for r0 in range(0, c, 8):
            bt, qt = b[r0:r0 + 8], qq[r0:r0 + 8]
            at = jnp.zeros((8, c), F32)
            for s in range(r0 + 8):
                diff = bt - b_scr[s:s + 1, :]
                if s >= r0:
                    diff = jnp.where(ti[0:8] + r0 >= s, diff, -jnp.inf)
                colv = jnp.sum(qt * jnp.exp(diff) * k_scr[s:s + 1, :], axis=1, keepdims=True)
                at = jnp.where(lane_c[0:8] == s, colv, at)
            a_tiles.append(at)
        a = jnp.concatenate(a_tiles, axis=0)
        s_old = s_ref[0, 0]
        o = (jnp.dot(a.astype(BF16), v.astype(BF16), preferred_element_type=F32)
             + jnp.dot((qq * jnp.exp(b)).astype(BF16), s_old.astype(BF16), preferred_element_type=F32))
        bl = b[c - 1:c, :]
        kdec = (kk * jnp.exp(bl - b)).astype(BF16)
        s_ref[0, 0] = (_row_to_col(jnp.exp(bl), eye) * s_old
                       + lax.dot_general(kdec, v.astype(BF16), (((0,), (0,)), ((), ())),
                                         preferred_element_type=F32))
        ya_ref[rows, :] = _hgrn_out(o, gon, hg_ref[rows, :])
        return carry

    lax.fori_loop(0, n_chunks, chunk, 0)


def _hgrn_prompt(z, lb, gon, batch, seq):
    zb = lambda name: pl.BlockSpec((seq, HG_K), lambda b, h, o=Z_OFF[name] // HG_K: (b, o + h))
    vec = pl.BlockSpec((1, HG_K), lambda b, h: (0, h))
    return pl.pallas_call(
        _hgrn_prompt_kernel,
        name="hgrn_prompt",
        grid=(batch, HG_HEADS),
        in_specs=[zb("hq"), zb("hf"), zb("hi"), zb("hgate"), vec, vec],
        out_specs=[pl.BlockSpec((seq, HG_V), lambda b, h: (b, h)),
                   pl.BlockSpec((1, 1, HG_K, HG_V), lambda b, h: (b, h, 0, 0))],
        out_shape=[jax.ShapeDtypeStruct((batch * seq, BRANCH_W), F32),
                   jax.ShapeDtypeStruct((batch, HG_HEADS, HG_K, HG_V), F32)],
        scratch_shapes=[pltpu.VMEM((HG_CHUNK, HG_K), F32), pltpu.VMEM((HG_CHUNK, HG_K), F32)],
        compiler_params=_cparams("parallel", "parallel"),
    )(z, z, z, z, lb.reshape(1, BRANCH_W), gon.reshape(1, BRANCH_W))


def _hgrn_sample_kernel(hq_ref, hf_ref, hi_ref, hg_ref, lb_ref, gon_ref, s0_ref, ya_ref, s_ref):
    n = hq_ref.shape[0]
    lb, gon = lb_ref[...], gon_ref[...]
    eye = lax.broadcasted_iota(I32, (HG_K, HG_K), 0) == lax.broadcasted_iota(I32, (HG_K, HG_K), 1)

    def seq(b, carry):
        row = pl.ds(b, 1)
        g, kk = _hgrn_gates(hf_ref[row, :], lb)
        hq = hq_ref[row, :]
        qq = hq * jax.nn.sigmoid(hq)
        s_new = (_row_to_col(jnp.exp(g), eye) * s0_ref[b, 0] + _row_to_col(kk, eye) * hi_ref[row, :])
        s_ref[b, 0] = s_new
        o = jnp.sum(_row_to_col(qq, eye) * s_new, axis=0, keepdims=True)
        ya_ref[row, :] = _hgrn_out(o, gon, hg_ref[row, :])
        return carry

    lax.fori_loop(0, n, seq, 0)


def _hgrn_sample(z, lb, gon, s0):
    n = z.shape[0]
    zb = lambda name: pl.BlockSpec((n, HG_K), lambda h, o=Z_OFF[name] // HG_K: (0, o + h))
    vec = pl.BlockSpec((1, HG_K), lambda h: (0, h))
    st = pl.BlockSpec((n, 1, HG_K, HG_V), lambda h: (0, h, 0, 0))
    return pl.pallas_call(
        _hgrn_sample_kernel,
        name="hgrn_sample",
        grid=(HG_HEADS,),
        in_specs=[zb("hq"), zb("hf"), zb("hi"), zb("hgate"), vec, vec, st],
        out_specs=[pl.BlockSpec((n, HG_V), lambda h: (0, h)), st],
        out_shape=[jax.ShapeDtypeStruct((n, BRANCH_W), F32),
                   jax.ShapeDtypeStruct((n, HG_HEADS, HG_K, HG_V), F32)],
        compiler_params=_cparams("parallel"),
    )(z, z, z, z, lb.reshape(1, BRANCH_W), gon.reshape(1, BRANCH_W), s0)


def _compress_consts(pe, w1, b1, w2):
    eye2 = jnp.eye(KV_HEADS, dtype=F32)
    w1r = w1.reshape(2, CMP_BLOCK, HEAD_DIM, HEAD_DIM)
    kron = lambda a: jnp.einsum('ij,ktdh->ktidjh', eye2, a).reshape(2, CMP_STRIDE, KV_W, KV_W)
    w1ab = jnp.concatenate([kron(w1r[:, :CMP_STRIDE]), kron(w1r[:, CMP_STRIDE:])], axis=-1).astype(BF16)
    pe8 = jnp.zeros((2, 8, CMP_BLOCK * HEAD_DIM), F32).at[:, 0].set(pe.reshape(2, -1)).astype(BF16)
    w1d = jnp.tile(w1, (1, 1, KV_HEADS)).astype(BF16)
    b1d = jnp.tile(b1, (1, KV_HEADS)).reshape(2, 1, KV_W).astype(F32)
    w2bd = jnp.einsum('ij,kdh->kidjh', eye2, w2).reshape(2, KV_W, KV_W).astype(BF16)
    return w1ab, pe8, w1d, b1d, w2bd


def _compress_body(load_chunk_rows, n_ch, w1ab_ref, pe_ref, w1d_ref, b1_ref, w2_ref, pb_scr, j):
    acc = jnp.zeros((n_ch, 2 * KV_W), F32)
    for t in range(CMP_STRIDE):
        acc = acc + load_chunk_rows(j, t, w1ab_ref)
    c0 = b1_ref[j] + jnp.dot(pe_ref[j], w1d_ref[j], preferred_element_type=F32)[0:1, :]
    pb_scr[0:n_ch, :] = acc[:, KV_W:]
    pb_scr[n_ch:n_ch + 8, :] = jnp.zeros((8, KV_W), F32)
    hid = acc[:, :KV_W] + pb_scr[pl.ds(1, n_ch), :] + c0
    hid = hid * jax.nn.sigmoid(hid)
    return jnp.dot(hid.astype(BF16), w2_ref[j], preferred_element_type=F32)


def _compress_kernel(ksrc_ref, vsrc_ref, w1ab_ref, pe_ref, w1d_ref, b1_ref, w2_ref, ck_ref, cv_ref, pb_scr):
    n_ch = ksrc_ref.shape[0] // CMP_STRIDE

    def load(j, t, w_ref):
        x = (ksrc_ref, vsrc_ref)[j][pl.ds(t, n_ch, stride=CMP_STRIDE), :]
        return jnp.dot(x.astype(BF16), w_ref[j, t], preferred_element_type=F32)

    for j, out_ref in enumerate((ck_ref, cv_ref)):
        out_ref[0] = _compress_body(load, n_ch, w1ab_ref, pe_ref, w1d_ref, b1_ref, w2_ref, pb_scr, j).astype(BF16)


def _compress(cmp32, consts, batch, seq):
    n_ch = seq // CMP_STRIDE
    full = lambda a: pl.BlockSpec(a.shape, lambda b: (0,) * a.ndim)
    out = pl.BlockSpec((1, n_ch, KV_W), lambda b: (b, 0, 0))
    return pl.pallas_call(
        _compress_kernel,
        name="compress",
        grid=(batch,),
        in_specs=[pl.BlockSpec((seq, KV_W), lambda b: (b, 0)), pl.BlockSpec((seq, KV_W), lambda b: (b, 1))]
        + [full(a) for a in consts],
        out_specs=[out, out],
        out_shape=[jax.ShapeDtypeStruct((batch, n_ch, KV_W), BF16)] * 2,
        scratch_shapes=[pltpu.VMEM((n_ch + 8, KV_W), F32)],
        compiler_params=_cparams("parallel"),
    )(cmp32, cmp32, *consts)


def _cover(n_rows, n_cmp, n_sel, n_cols):
    c_start = np.arange(n_rows)[:, None] * CMP_STRIDE
    s_start = np.arange(n_cols)[None, :] * SEL_BLOCK
    ov = np.clip(np.minimum(c_start + CMP_BLOCK, s_start + SEL_BLOCK) - np.maximum(c_start, s_start), 0, None)
    ov = ov.astype(np.float32) / CMP_BLOCK
    ov[n_cmp:, :] = 0.0
    ov[:, n_sel:] = 0.0
    return jnp.asarray(ov, BF16)


def _kth_largest(count_ge, max_lt, lo, hi, n_valid, k):
    top = hi
    top_ok = count_ge(top) >= k

    def halve(_, c):
        lo, hi = c
        mid = 0.5 * lo + 0.5 * hi
        ge = count_ge(mid) >= k
        return jnp.where(ge, mid, lo), jnp.where(ge, hi, mid)

    def check(lo, hi):
        t = max_lt(hi)
        done = (count_ge(t) >= k) | top_ok | (n_valid < k)
        return t, jnp.where(done, 1.0, 0.0)

    def refine(c):
        it, lo, hi, _, _ = c
        lo, hi = lax.fori_loop(0, 8, halve, (lo, hi))
        return (it + 1, lo, hi) + check(lo, hi)

    lo, hi = lax.fori_loop(0, 24, halve, (lo, hi))
    state = (jnp.int32(0), lo, hi) + check(lo, hi)
    _, _, _, t, _ = lax.while_loop(lambda c: (c[0] < 40) & (jnp.min(c[4]) < 0.5), refine, state)
    return jnp.where(n_valid < k, -jnp.inf, jnp.where(top_ok, top, t))


def _split2(x):
    hi = x.astype(BF16)
    return hi, (x - hi.astype(F32)).astype(BF16)


def _nt(a, b):
    return lax.dot_general(a, b, (((1,), (1,)), ((), ())), preferred_element_type=F32)


def _group_queries(q_ref, g):
    return jnp.concatenate([q_ref[:, h * LANES:(h + 1) * LANES] for h in range(g * GROUP, (g + 1) * GROUP)], axis=0)


def _chain_init(st_ref, c):
    st_ref[c, 0] = jnp.full((GROUP * QB, LANES), NEG, F32)
    st_ref[c, 1] = jnp.zeros((GROUP * QB, LANES), F32)
    st_ref[c, 2] = jnp.zeros((GROUP * QB, LANES), F32)


def _chain_step(st_ref, c, q4, kblk, vblk, bias, ones):
    s = _nt(q4, kblk) + jnp.concatenate([bias] * GROUP, axis=0)
    m_old = st_ref[c, 0]
    m_new = jnp.maximum(m_old, jnp.max(s, axis=1, keepdims=True))
    alpha = jnp.exp(m_old - m_new)
    p = jnp.exp(s - m_new).astype(BF16)
    st_ref[c, 0] = m_new
    st_ref[c, 1] = alpha * st_ref[c, 1] + jnp.dot(p, ones, preferred_element_type=F32)
    st_ref[c, 2] = alpha * st_ref[c, 2] + jnp.dot(p, vblk, preferred_element_type=F32)


def _chain_result(st_ref, c, hh):
    rows = slice(hh * QB, (hh + 1) * QB)
    return st_ref[c, 2, rows, :] / st_ref[c, 1, rows, :]


def _pair_to_lanes(o_even, o_odd, h_even, lane):
    left = o_even if h_even // GROUP == 0 else pltpu.roll(o_even, HEAD_DIM, 1)
    right = o_odd if (h_even + 1) // GROUP == 1 else pltpu.roll(o_odd, HEAD_DIM, 1)
    return jnp.where(lane < HEAD_DIM, left, right)


def _topn_mask(v, n_cand, n_top, lane):
    rank = jnp.zeros(v.shape, F32)
    for i in range(n_cand):
        ci = v[:, i:i + 1]
        rank = rank + jnp.where((ci > v) | ((ci == v) & (lane > i)), 1.0, 0.0)
    return jnp.where((rank < n_top) & (lane < n_cand), 1.0, 0.0)


def _attn_prompt_kernel(qd_ref, qi_ref, qn_ref, iw_ref, nbg_ref,
                        kd_ref, vd_ref, ki_ref, ks_ref, vs_ref, kw_ref, vw_ref, ck_ref, cv_ref, cover_ref,
                        yb_ref, yc_ref, key_scr, db_scr, sb_scr, acc_scr, st_scr, *, n_top, n_sel):
    i = pl.program_id(1)
    nkb = i + 1
    row = lax.broadcasted_iota(I32, (QB, LANES), 0)
    lane = lax.broadcasted_iota(I32, (QB, LANES), 1)
    qpos = i * QB + row

    for h in range(IDX_HEADS):
        acc_scr[h] = jnp.broadcast_to(iw_ref[:, h:h + 1], (QB, LANES))

    def idx_blk(kb, c):
        kblk = ki_ref[pl.ds(pl.multiple_of(kb * QB, QB), QB), :]
        score = jnp.zeros((QB, LANES), F32)
        for h in range(IDX_HEADS):
            score = score + acc_scr[h] * jnp.maximum(_nt(qi_ref[:, h * LANES:(h + 1) * LANES], kblk), 0.0)
        key_scr[kb] = jnp.where(kb * QB + lane <= qpos, score, -jnp.inf)
        return c

    lax.fori_loop(0, nkb, idx_blk, 0)

    def sweep(fn, init, reduce):
        acc = lax.fori_loop(0, nkb, lambda kb, a: fn(a, key_scr[kb]), jnp.full((QB, LANES), init, F32))
        return reduce(acc, axis=1, keepdims=True)

    count_ge = lambda x: sweep(lambda a, k: a + jnp.where(k >= x, 1.0, 0.0), 0.0, jnp.sum)
    max_lt = lambda x: sweep(lambda a, k: jnp.maximum(a, jnp.where(k < x, k, -jnp.inf)), -jnp.inf, jnp.max)
    lo = sweep(lambda a, k: jnp.minimum(a, jnp.where(k == -jnp.inf, jnp.inf, k)), jnp.inf, jnp.min)
    hi = sweep(jnp.maximum, -jnp.inf, jnp.max)
    n_valid = (qpos[:, 0:1] + 1).astype(F32)
    thr = _kth_largest(count_ge, max_lt, lo, hi, n_valid, n_top)
    need = n_top - sweep(lambda a, k: a + jnp.where(k > thr, 1.0, 0.0), 0.0, jnp.sum)
    upper = jnp.where(row <= lane, 1.0, 0.0).astype(BF16)

    def mask_blk(kb, before):
        key = key_scr[kb]
        eq = key == thr
        pref = jnp.dot(jnp.where(eq, 1.0, 0.0).astype(BF16), upper, preferred_element_type=F32) + before
        sel = ((key > thr) | (eq & (pref <= need))) & (kb * QB + lane <= qpos)
        db_scr[kb] = jnp.where(sel, 0.0, NEG)
        return before + jnp.sum(jnp.where(eq, 1.0, 0.0), axis=1, keepdims=True)

    lax.fori_loop(0, nkb, mask_blk, jnp.zeros((QB, 1), F32))

    gb = jax.nn.sigmoid(nbg_ref[...])
    gcol = lambda h, r: gb[:, 3 * h + r:3 * h + r + 1]
    ck, cv = ck_ref[0], cv_ref[0]
    n_ch = ck.shape[0]
    cl = lax.broadcasted_iota(I32, (QB, n_ch), 1)
    cvalid = cl * CMP_STRIDE + (CMP_BLOCK - 1) <= i * QB + lax.broadcasted_iota(I32, (QB, n_ch), 0)
    for g in range(KV_HEADS):
        imp = jnp.zeros((QB, n_ch), F32)
        for h in range(g * GROUP, (g + 1) * GROUP):
            s = jnp.where(cvalid, _nt(qn_ref[:, h * LANES:(h + 1) * LANES], ck), NEG)
            e = jnp.exp(s - jnp.max(s, axis=1, keepdims=True))
            p = jnp.where(cvalid, e / jnp.sum(e, axis=1, keepdims=True), 0.0)
            acc_scr[h] = gcol(h, 0) * jnp.dot(p.astype(BF16), cv, preferred_element_type=F32)
            imp = imp + p
        ih, il = _split2(imp)
        impj = (jnp.dot(ih, cover_ref[...], preferred_element_type=F32)
                + jnp.dot(il, cover_ref[...], preferred_element_type=F32))
        tb = qpos >> 6
        forced = (lane == 0) | (lane == tb) | (lane == tb - 1)
        v = jnp.where(lane * SEL_BLOCK <= qpos, jnp.where(forced, FORCE, impj), NEG)
        sel16 = _topn_mask(v, n_sel, min(SEL_TOPN, n_sel), lane).astype(BF16)

        def sb_blk(kb, c, g=g, sel16=sel16):
            expand = jnp.where(row == 2 * kb + (lane >> 6), 1.0, 0.0).astype(BF16)
            tok = jnp.dot(sel16, expand, preferred_element_type=F32)
            sb_scr[g, kb] = jnp.where((tok > 0.5) & (kb * QB + lane <= qpos), 0.0, NEG)
            return c

        lax.fori_loop(0, nkb, sb_blk, 0)

    ones = jnp.ones((LANES, LANES), BF16)
    for c in range(3 * KV_HEADS):
        _chain_init(st_scr, c)

    def blk_dsa_sel(kb, c):
        r = pl.ds(pl.multiple_of(kb * QB, QB), QB)
        for g in range(KV_HEADS):
            _chain_step(st_scr, g, _group_queries(qd_ref, g), kd_ref[r, :], vd_ref[r, :], db_scr[kb], ones)
            _chain_step(st_scr, KV_HEADS + g, _group_queries(qn_ref, g), ks_ref[r, :], vs_ref[r, :], sb_scr[g, kb],
                        ones)
        return c

    def blk_win(kb, c):
        r = pl.ds(pl.multiple_of(kb * QB, QB), QB)
        kpos = kb * QB + lane
        bias = jnp.where((kpos <= qpos) & (qpos - kpos < WINDOW), 0.0, NEG)
        for g in range(KV_HEADS):
            _chain_step(st_scr, 2 * KV_HEADS + g, _group_queries(qn_ref, g), kw_ref[r, :], vw_ref[r, :], bias, ones)
        return c

    lax.fori_loop(0, nkb, blk_dsa_sel, 0)
    lax.fori_loop(jnp.maximum(i - WINDOW // QB, 0), nkb, blk_win, 0)

    for j in range(N_HEADS // 2):
        cols = slice(j * LANES, (j + 1) * LANES)
        o_d, o_c = [], []
        for h in (2 * j, 2 * j + 1):
            g, hh = divmod(h, GROUP)
            o_d.append(_chain_result(st_scr, g, hh))
            o_c.append(acc_scr[h] + gcol(h, 1) * _chain_result(st_scr, KV_HEADS + g, hh)
                       + gcol(h, 2) * _chain_result(st_scr, 2 * KV_HEADS + g, hh))
        yb_ref[:, cols] = _pair_to_lanes(o_d[0], o_d[1], 2 * j, lane)
        yc_ref[:, cols] = _pair_to_lanes(o_c[0], o_c[1], 2 * j, lane)


def _attn_prompt(qd, qi, qn, iw, z, kv16, ck, cv, batch, seq):
    nq = seq // QB
    n_ch = seq // CMP_STRIDE
    n_sel = seq // SEL_BLOCK
    n_cmp = (seq - CMP_BLOCK) // CMP_STRIDE + 1
    cover = _cover(n_ch, n_cmp, n_sel, LANES)
    qspec = pl.BlockSpec((QB, N_HEADS * LANES), lambda b, i: (b * nq + i, 0))
    zspec = lambda name, w: pl.BlockSpec((QB, w), lambda b, i, o=Z_OFF[name] // w: (b * nq + i, o))
    kvspec = lambda blk: pl.BlockSpec((seq, LANES), lambda b, i: (b, blk))
    cspec = pl.BlockSpec((1, n_ch, KV_W), lambda b, i: (b, 0, 0))
    yspec = pl.BlockSpec((QB, BRANCH_W), lambda b, i: (b * nq + i, 0))
    kern = functools.partial(_attn_prompt_kernel, n_top=min(DSA_TOPK, seq // 4), n_sel=n_sel)
    return pl.pallas_call(
        kern,
        name="attn_prompt",
        grid=(batch, nq),
        in_specs=[qspec, qspec, qspec, pl.BlockSpec((QB, LANES), lambda b, i: (b * nq + i, 0)),
                  zspec("nbg", LANES),
                  kvspec(KV_KD), kvspec(KV_VD), kvspec(KV_KI), kvspec(KV_KS), kvspec(KV_VS), kvspec(KV_KW),
                  kvspec(KV_VW), cspec, cspec, pl.BlockSpec(cover.shape, lambda b, i: (0, 0))],
        out_specs=[yspec, yspec],
        out_shape=[jax.ShapeDtypeStruct((batch * seq, BRANCH_W), F32)] * 2,
        scratch_shapes=[pltpu.VMEM((nq, QB, LANES), F32), pltpu.VMEM((nq, QB, LANES), F32),
                        pltpu.VMEM((KV_HEADS, nq, QB, LANES), F32), pltpu.VMEM((N_HEADS, QB, LANES), F32),
                        pltpu.VMEM((3 * KV_HEADS, 3, GROUP * QB, LANES), F32)],
        compiler_params=_cparams("parallel", "arbitrary"),
    )(qd, qi, qn, iw, z, kv16, kv16, kv16, kv16, kv16, kv16, kv16, ck, cv, cover)


R_DK, R_DV, R_IK, R_CK, R_CV, R_SK, R_SV = 0, 128, 256, 320, 448, 576, 704
ROW_GROUPS = ((R_IK, R_CK), (R_DK, R_IK), (R_CK, R_SK), (R_SK, CACHE_W))
G_IDX, G_DSA, G_CMP, G_SEL = range(4)


def _key_chunk(n_blocks):
    return QB * max(d for d in range(1, 17) if n_blocks % d == 0)


def _attn_sample_kernel(pt_ref, cache_ref, new_ref, qd_ref, qi_ref, qn_ref, iw_ref, nbg_ref, win_ref,
                        w1ab_ref, pe_ref, w1d_ref, b1_ref, w2_ref, cover_ref,
                        od_ref, oc_ref, buf, cmp_scr, pb_scr, tok_scr, pref_scr, sems, *, layer, past, n_top):
    b = pl.program_id(0)
    n_pages = past // PAGE_SIZE
    nk = past + QB
    kc = _key_chunk(nk // QB)
    n_ch = past // CMP_STRIDE
    n_sel = -(-(past + 1) // SEL_BLOCK)
    n_selp = cover_ref.shape[1]

    def group_copy(g, p):
        r0, r1 = ROW_GROUPS[g]
        cols = pl.ds(pl.multiple_of(p * PAGE_SIZE, PAGE_SIZE), PAGE_SIZE)
        return pltpu.make_async_copy(cache_ref.at[layer, pt_ref[b, p], pl.ds(r0, r1 - r0), :],
                                     buf.at[pl.ds(r0, r1 - r0), cols], sems.at[g])

    def each_page(fn):
        def body(p, c):
            fn(p)
            return c
        lax.fori_loop(0, n_pages, body, 0)

    for g in range(len(ROW_GROUPS)):
        each_page(lambda p, g=g: group_copy(g, p).start())
    buf[:, pl.ds(past, QB)] = jnp.zeros((CACHE_W, QB), F32)
    buf[:, pl.ds(past, 1)] = new_ref[0]
    arrived = lambda g: each_page(lambda p: group_copy(g, p).wait())

    kpos1 = lax.broadcasted_iota(I32, (1, nk), 1)
    in_range = kpos1 <= past
    lo_heads = lax.broadcasted_iota(I32, (N_HEADS, HEAD_DIM), 0) < GROUP

    def scores(q, r0):
        n = q.shape[1]
        return jnp.concatenate([jnp.dot(q, buf[r0:r0 + n, c:c + kc], preferred_element_type=F32)
                                for c in range(0, nk, kc)], axis=1)

    def weighted(p, r0):
        return sum(_nt(p[:, c:c + kc], buf[r0:r0 + KV_W, c:c + kc]) for c in range(0, nk, kc))

    def softmax(s):
        e = jnp.exp(s - jnp.max(s, axis=1, keepdims=True))
        return e / jnp.sum(e, axis=1, keepdims=True)

    compact = lambda o: jnp.where(lo_heads, o[:, :HEAD_DIM], o[:, HEAD_DIM:])

    arrived(G_IDX)
    sc = jnp.sum(iw_ref[0] * jnp.maximum(scores(qi_ref[0], R_IK), 0.0), axis=0, keepdims=True)
    key = jnp.where(in_range, sc, -jnp.inf)
    count = lambda pred: jnp.sum(jnp.where(pred, 1.0, 0.0), axis=1, keepdims=True)
    thr = _kth_largest(lambda x: count(key >= x),
                       lambda x: jnp.max(jnp.where(key < x, key, -jnp.inf), axis=1, keepdims=True),
                       jnp.min(jnp.where(in_range, sc, jnp.inf), axis=1, keepdims=True),
                       jnp.max(key, axis=1, keepdims=True), jnp.full((1, 1), past + 1.0, F32), n_top)
    need = n_top - count(key > thr)
    eq8 = jnp.where(jnp.broadcast_to(key == thr, (N_HEADS, nk)), 1.0, 0.0)
    upper = jnp.where(lax.broadcasted_iota(I32, (LANES, LANES), 0) <= lax.broadcasted_iota(I32, (LANES, LANES), 1),
                      1.0, 0.0).astype(BF16)
    before = jnp.zeros((N_HEADS, 1), F32)
    for c in range(0, nk, LANES):
        pref = jnp.dot(eq8[:, c:c + LANES].astype(BF16), upper, preferred_element_type=F32) + before
        pref_scr[:, c:c + LANES] = pref
        before = before + jnp.sum(eq8[:, c:c + LANES], axis=1, keepdims=True)
    sel = ((key > thr) | ((key == thr) & (pref_scr[0:1, :] <= need))) & in_range
    arrived(G_DSA)
    p = softmax(scores(qd_ref[0], R_DK) + jnp.where(sel, 0.0, NEG))
    od_ref[0] = compact(weighted(p, R_DV))

    arrived(G_CMP)

    def to_token_major(p):
        cols = pl.ds(pl.multiple_of(p * PAGE_SIZE, PAGE_SIZE), PAGE_SIZE)
        for j, r0 in enumerate((R_CK, R_CV)):
            cmp_scr[j, cols, :] = buf[r0:r0 + KV_W, cols].T

    each_page(to_token_major)

    def load(j, t, w_ref):
        x = cmp_scr[j, pl.ds(t, n_ch, stride=CMP_STRIDE), :]
        return jnp.dot(x.astype(BF16), w_ref[j, t], preferred_element_type=F32)

    ck = _compress_body(load, n_ch, w1ab_ref, pe_ref, w1d_ref, b1_ref, w2_ref, pb_scr, 0).astype(BF16)
    cv = _compress_body(load, n_ch, w1ab_ref, pe_ref, w1d_ref, b1_ref, w2_ref, pb_scr, 1).astype(BF16)

    gb = jax.nn.sigmoid(nbg_ref[0])
    qn = qn_ref[0]
    cvalid = lax.broadcasted_iota(I32, (N_HEADS, n_ch), 1) * CMP_STRIDE + (CMP_BLOCK - 1) <= past
    s = jnp.where(cvalid, _nt(qn.astype(BF16), ck), NEG)
    e = jnp.exp(s - jnp.max(s, axis=1, keepdims=True))
    pc = jnp.where(cvalid, e / jnp.sum(e, axis=1, keepdims=True), 0.0)
    o_cmp = compact(jnp.dot(pc.astype(BF16), cv, preferred_element_type=F32))
    imp = jnp.where(lax.broadcasted_iota(I32, (N_HEADS, n_ch), 0) < GROUP,
                    jnp.sum(pc[:GROUP], axis=0, keepdims=True), jnp.sum(pc[GROUP:], axis=0, keepdims=True))
    ih, il = _split2(imp)
    impj = (jnp.dot(ih, cover_ref[...], preferred_element_type=F32)
            + jnp.dot(il, cover_ref[...], preferred_element_type=F32))
    lane_s = lax.broadcasted_iota(I32, (N_HEADS, n_selp), 1)
    tb = past // SEL_BLOCK
    forced = (lane_s == 0) | (lane_s == tb) | (lane_s == tb - 1)
    v = jnp.where(lane_s * SEL_BLOCK <= past, jnp.where(forced, FORCE, impj), NEG)
    sel8 = _topn_mask(v, n_sel, min(SEL_TOPN, n_sel), lane_s)
    half = lax.broadcasted_iota(I32, (N_HEADS, LANES), 1) < SEL_BLOCK
    for c in range(nk // LANES):
        tok_scr[:, c * LANES:(c + 1) * LANES] = jnp.where(half, sel8[:, 2 * c:2 * c + 1], sel8[:, 2 * c + 1:2 * c + 2])

    arrived(G_SEL)
    p = softmax(jnp.where((tok_scr[...] > 0.5) & in_range, scores(qn, R_SK), NEG))
    o_sel = compact(weighted(p, R_SV))

    pw = softmax(_nt(qn, win_ref[0, :, 0:KV_W]))
    o_win = compact(jnp.dot(pw, win_ref[0, :, KV_W:2 * KV_W], preferred_element_type=F32))
    oc_ref[0] = gb[:, 0:1] * o_cmp + gb[:, 1:2] * o_sel + gb[:, 2:3] * o_win


def _attn_sample(page_table, cache_t, layer, new_rows, qd, qi, qn, iw, nbg, win, cmp_consts):
    nb, n_pages = page_table.shape
    past = n_pages * PAGE_SIZE
    nk = past + QB
    n_ch = past // CMP_STRIDE
    n_sel = -(-(past + 1) // SEL_BLOCK)
    n_selp = -(-n_sel // LANES) * LANES
    n_cmp = (past + 1 - CMP_BLOCK) // CMP_STRIDE + 1
    cover = _cover(n_ch, n_cmp, n_sel, n_selp)
    consts = tuple(cmp_consts) + (cover,)
    per_seq = lambda a: pl.BlockSpec((1,) + a.shape[1:], lambda b, pt: (b,) + (0,) * (a.ndim - 1))
    full = lambda a: pl.BlockSpec(a.shape, lambda b, pt: (0,) * a.ndim)
    args = (new_rows, qd, qi, qn, iw, nbg, win)
    kern = functools.partial(_attn_sample_kernel, layer=layer, past=past, n_top=min(DSA_TOPK, (past + 1) // 4))
    out = pl.BlockSpec((1, N_HEADS, HEAD_DIM), lambda b, pt: (b, 0, 0))
    return pl.pallas_call(
        kern,
        name="attn_sample",
        grid_spec=pltpu.PrefetchScalarGridSpec(
            num_scalar_prefetch=1,
            grid=(nb,),
            in_specs=[pl.BlockSpec(memory_space=pl.ANY)] + [per_seq(a) for a in args] + [full(a) for a in consts],
            out_specs=[out, out],
            scratch_shapes=[pltpu.VMEM((CACHE_W, nk), F32), pltpu.VMEM((2, past, KV_W), F32),
                            pltpu.VMEM((n_ch + 8, KV_W), F32),
                            pltpu.VMEM((N_HEADS, nk), F32), pltpu.VMEM((N_HEADS, nk), F32),
                            pltpu.SemaphoreType.DMA((len(ROW_GROUPS),))]),
        out_shape=[jax.ShapeDtypeStruct((nb, N_HEADS, HEAD_DIM), F32)] * 2,
        compiler_params=_cparams("arbitrary"),
    )(page_table, cache_t, *args, *consts)


def _merge_kernel(x_ref, ya_ref, yb_ref, yc_ref, dg_ref, ng_ref, ga_ref, gb_ref, gc_ref, wb_ref, wo_ref, y_ref):
    silu = lambda v: v * jax.nn.sigmoid(v)
    ys = (ya_ref[...], yb_ref[...] * silu(dg_ref[...]), yc_ref[...] * silu(ng_ref[...]))
    m = None
    for n, (y, g) in enumerate(zip(ys, (ga_ref, gb_ref, gc_ref))):
        pr = jax.nn.sigmoid(g[...]) * jnp.dot(y.astype(BF16), wb_ref[n], preferred_element_type=F32)
        m = pr if m is None else m + pr
    y_ref[...] = x_ref[...] + jnp.dot(m.astype(BF16), wo_ref[...], preferred_element_type=F32)


def _merge(x2, ya, yb, yc, z, wb16, wo16):
    t, d = x2.shape
    tm = min(t, 512)
    yspec = pl.BlockSpec((tm, BRANCH_W), lambda i: (i, 0))
    gspec = lambda name, w=d: pl.BlockSpec((tm, w), lambda i, o=Z_OFF[name] // w: (i, o))
    return pl.pallas_call(
        _merge_kernel,
        name="merge",
        grid=(t // tm,),
        in_specs=[pl.BlockSpec((tm, d), lambda i: (i, 0)), yspec, yspec, yspec,
                  gspec("dgate", BRANCH_W), gspec("ngate", BRANCH_W), gspec("ma"), gspec("mb"), gspec("mc"),
                  pl.BlockSpec(wb16.shape, lambda i: (0, 0, 0)), pl.BlockSpec(wo16.shape, lambda i: (0, 0))],
        out_specs=pl.BlockSpec((tm, d), lambda i: (i, 0)),
        out_shape=jax.ShapeDtypeStruct((t, d), F32),
        compiler_params=_cparams("parallel"),
    )(x2, ya, yb, yc, z, z, z, z, z, wb16, wo16)


def _layer_prompt(x, lb, norm_g, w16, gon, prep_consts, cmp_consts, wb16, wo16):
    batch, seq, d = x.shape
    x2 = x.reshape(batch * seq, d)
    z = _proj(x2, norm_g, w16)
    tabs = _rope_tables(jnp.arange(seq, dtype=I32))
    rows, wrows, cmp32, qd, qi, qn, iw, kv16 = _prep(z, tabs, *prep_consts, min(seq, 256))
    ya, s_new = _hgrn_prompt(z, lb, gon, batch, seq)
    ck, cv = _compress(cmp32, cmp_consts, batch, seq)
    yb, yc = _attn_prompt(qd, qi, qn, iw, z, kv16, ck, cv, batch, seq)
    y = _merge(x2, ya, yb, yc, z, wb16, wo16).reshape(batch, seq, d)
    wrows = wrows.reshape(batch, seq, WIN_W)
    return y, rows.reshape(batch, seq, CACHE_W), wrows[:, -min(WINDOW, seq):], s_new


def _layer_sample(x, lb, norm_g, w16, gon, prep_consts, cmp_consts, wb16, wo16,
                  page_table, cache_kv, layer, win_rows, s0):
    nb, one, d = x.shape
    assert one == 1 and win_rows.shape[1] == WINDOW
    past = page_table.shape[1] * PAGE_SIZE
    x2 = x.reshape(nb, d)
    z = _proj(x2, norm_g, w16)
    tabs = _rope_tables(jnp.full((nb,), past, I32))
    rows, wrows, _, qd, qi, qn, iw, _ = _prep(z, tabs, *prep_consts, nb)
    ya, s_new = _hgrn_sample(z, lb, gon, s0)
    win = jnp.concatenate([win_rows[:, 1:], wrows[:, None, :]], axis=1)
    heads = lambda q: q.astype(F32).reshape(nb, N_HEADS, LANES)
    nbg = z[:, Z_OFF["nbg"]:Z_OFF["nbg"] + 3 * N_HEADS].reshape(nb, N_HEADS, 3)
    od, oc = _attn_sample(page_table, cache_kv, layer, rows.reshape(nb, CACHE_W, 1), heads(qd),
                          heads(qi)[:, :, :IDX_DIM], heads(qn), iw[:, :IDX_HEADS].reshape(nb, IDX_HEADS, 1), nbg, win,
                          cmp_consts)
    y = _merge(x2, ya, od.reshape(nb, BRANCH_W), oc.reshape(nb, BRANCH_W), z, wb16, wo16)
    return y.reshape(nb, 1, d), rows.reshape(nb, 1, CACHE_W), win, s_new


def kernel(x_prompt, x_sample, cache_kv, page_table, state_win, state_hgrn, norm_g, w_in,
           hgrn_lb_logits, hgrn_onorm_g, dsa_qk_norm_g, dsa_idx_k_norm_g, nsa_qk_norm_g,
           nsa_cmp_pe, nsa_cmp_w1, nsa_cmp_b1, nsa_cmp_w2, w_branch, w_out):
    lb_soft = jax.nn.softmax(hgrn_lb_logits.astype(F32), axis=0)
    lbs = jnp.cumsum(lb_soft, axis=0) - lb_soft[0]
    xp, xs = x_prompt, x_sample
    cache_t = jnp.swapaxes(cache_kv, 2, 3)
    outs = [[] for _ in range(6)]
    for l in range(w_in.shape[0]):
        shared = (lbs[l], norm_g[l], _perm_w_in(w_in[l]), hgrn_onorm_g[l],
                  _prep_consts(dsa_qk_norm_g[l], dsa_idx_k_norm_g[l], nsa_qk_norm_g[l]),
                  _compress_consts(nsa_cmp_pe[l], nsa_cmp_w1[l], nsa_cmp_b1[l], nsa_cmp_w2[l]),
                  w_branch[l].astype(BF16), w_out[l].astype(BF16))
        xs, *s_out = _layer_sample(xs, *shared, page_table, cache_t, l, state_win[l], state_hgrn[l])
        xp, *p_out = _layer_prompt(xp, *shared)
        for acc, o in zip(outs, p_out + s_out):
            acc.append(o)
    return (xp, xs) + tuple(jnp.stack(o) for o in outs)
```

```python
import functools

import jax
import jax.numpy as jnp
from jax import lax
import numpy as np
from jax.experimental import pallas as pl
from jax.experimental.pallas import tpu as pltpu

F32 = jnp.float32
BF16 = jnp.bfloat16
I32 = jnp.int32

D_MODEL = 1024
PAGE_SIZE = 128
BRANCH_W = D_MODEL // 2
HEAD_DIM = 64
ROPE_THETA = 500000.0
EPS = 1e-6
NEG = -1e30
FORCE = 1e9
HG_HEADS = 4
HG_K = BRANCH_W // HG_HEADS
HG_V = BRANCH_W // HG_HEADS
HG_CHUNK = 64
HG_SUB = 16
N_HEADS = BRANCH_W // HEAD_DIM
KV_HEADS = 2
GROUP = N_HEADS // KV_HEADS
IDX_HEADS = 8
IDX_DIM = 64
DSA_TOPK = 256
CMP_BLOCK = 32
CMP_STRIDE = 16
SEL_BLOCK = 64
SEL_TOPN = 16
WINDOW = 512
KV_W = KV_HEADS * HEAD_DIM
CACHE_W = 2 * KV_W + IDX_DIM + 4 * KV_W
WIN_W = 2 * KV_W
LANES = 128
QB = 128
INT_MIN = np.int32(-2 ** 31)
VMEM_LIMIT = 56 * 1024 * 1024

IN_SPLITS = (
    ("hq", 512), ("hf", 512), ("hi", 512), ("hgate", 512),
    ("dq", 512), ("dk", 128), ("dv", 128), ("iq", 512), ("ik", 64), ("iw", 8), ("dgate", 512),
    ("nq", 512), ("nck", 128), ("ncv", 128), ("nsk", 128), ("nsv", 128), ("nwk", 128), ("nwv", 128),
    ("nbg", 24), ("ngate", 512), ("ma", 1024), ("mb", 1024), ("mc", 1024),
)
Z_ORDER = ("ma", "mb", "mc", "hq", "hf", "hi", "hgate", "dq", "iq", "nq", "dgate", "ngate",
           "dk", "dv", "ik", "iw", "nck", "ncv", "nsk", "nsv", "nwk", "nwv", "nbg")
Z_COLS = 9216


def _z_layout():
    widths = dict(IN_SPLITS)
    src, off = {}, 0
    for name, w in IN_SPLITS:
        src[name] = off
        off += w
    dst, off = {}, 0
    for name in Z_ORDER:
        dst[name] = off
        off += -(-widths[name] // LANES) * LANES
    assert off <= Z_COLS
    idx = np.zeros((Z_COLS,), np.int32)
    valid = np.zeros((Z_COLS,), np.float32)
    for name in Z_ORDER:
        w = widths[name]
        idx[dst[name]:dst[name] + w] = src[name] + np.arange(w)
        valid[dst[name]:dst[name] + w] = 1.0
    return dst, idx, valid


Z_OFF, Z_SRC_IDX, Z_VALID = _z_layout()


def _cparams(*sem):
    return pltpu.CompilerParams(dimension_semantics=sem, vmem_limit_bytes=VMEM_LIMIT)


def _proj_kernel(x_ref, g_ref, w_ref, z_ref, h_scr):
    @pl.when(pl.program_id(1) == 0)
    def _():
        x = x_ref[...]
        ms = jnp.mean(x * x, axis=-1, keepdims=True)
        h_scr[...] = (x * lax.rsqrt(ms + EPS) * g_ref[...]).astype(BF16)

    z_ref[...] = jnp.dot(h_scr[...], w_ref[...], preferred_element_type=F32)


def _proj(x2, g, w16):
    t, d = x2.shape
    tm = min(t, 1024)
    tn = 1536
    assert t % tm == 0
    return pl.pallas_call(
        _proj_kernel,
        name="proj",
        grid=(t // tm, Z_COLS // tn),
        in_specs=[pl.BlockSpec((tm, d), lambda i, j: (i, 0)),
                  pl.BlockSpec((1, d), lambda i, j: (0, 0)),
                  pl.BlockSpec((d, tn), lambda i, j: (0, j))],
        out_specs=pl.BlockSpec((tm, tn), lambda i, j: (i, j)),
        out_shape=jax.ShapeDtypeStruct((t, Z_COLS), F32),
        scratch_shapes=[pltpu.VMEM((tm, d), BF16)],
        compiler_params=_cparams("parallel", "arbitrary"),
    )(x2, g.reshape(1, d), w16)


KV_KD, KV_VD, KV_KI, KV_KS, KV_VS, KV_KW, KV_VW = range(7)
G_DQ, G_DK, G_IK, G_NQ, G_NCK, G_NSK, G_NWK = range(7)


def _head_rms(x, gain, bd):
    x2 = x * x
    hi = x2.astype(BF16)
    lo = (x2 - hi.astype(F32)).astype(BF16)
    ms = jnp.dot(hi, bd, preferred_element_type=F32) + jnp.dot(lo, bd, preferred_element_type=F32)
    return x * lax.rsqrt(ms + EPS) * gain


def _rope128(x, cosf, sina, sinb):
    return x * cosf + pltpu.roll(x, LANES - 8, 1) * sina + pltpu.roll(x, 8, 1) * sinb


def _prep_kernel(dq_ref, iq_ref, nq_ref, s0_ref, s1_ref, s2_ref, cos_ref, sina_ref, sinb_ref, g_ref, bd_ref,
                 rows_ref, wrows_ref, cmp_ref, qd_ref, qi_ref, qn_ref, iw_ref, kv_ref):
    cosf, sina, sinb = cos_ref[...], sina_ref[...], sinb_ref[...]
    bd = bd_ref[...]
    lane = lax.broadcasted_iota(I32, cosf.shape, 1)
    lo_half = lane < HEAD_DIM
    rope = lambda v: _rope128(v, cosf, sina, sinb)
    gain = lambda r: g_ref[r:r + 1, :]

    def put_q(src_ref, out_ref, g_row, scale, target_of_head):
        for j in range(N_HEADS // 2):
            x = src_ref[:, j * LANES:(j + 1) * LANES]
            if g_row is not None:
                x = _head_rms(x, gain(g_row), bd)
            x = rope(x) * scale
            xr = pltpu.roll(x, HEAD_DIM, 1)
            for half in range(2):
                h = 2 * j + half
                tgt = target_of_head(h)
                v = x if tgt == half else xr
                keep = lo_half if tgt == 0 else jnp.logical_not(lo_half)
                out_ref[:, h * LANES:(h + 1) * LANES] = jnp.where(keep, v, 0.0).astype(BF16)

    put_q(dq_ref, qd_ref, G_DQ, HEAD_DIM ** -0.5, lambda h: h // GROUP)
    put_q(iq_ref, qi_ref, None, 1.0, lambda h: 0)
    put_q(nq_ref, qn_ref, G_NQ, HEAD_DIM ** -0.5, lambda h: h // GROUP)

    dk = rope(_head_rms(s0_ref[:, 0:128], gain(G_DK), bd))
    dv = s0_ref[:, 128:256]
    ik = rope(_head_rms(s0_ref[:, 256:384], gain(G_IK), bd))
    iw_ref[...] = s0_ref[:, 384:512] * (IDX_HEADS ** -0.5 * IDX_DIM ** -0.5)
    nck = rope(_head_rms(s1_ref[:, 0:128], gain(G_NCK), bd))
    ncv = s1_ref[:, 128:256]
    nsk = rope(_head_rms(s1_ref[:, 256:384], gain(G_NSK), bd))
    nsv = s1_ref[:, 384:512]
    nwk = rope(_head_rms(s2_ref[:, 0:128], gain(G_NWK), bd))
    nwv = s2_ref[:, 128:256]

    rows_ref[:, 0:128] = dk
    rows_ref[:, 128:256] = dv
    rows_ref[:, 256:320] = ik[:, 0:64]
    rows_ref[:, 320:448] = nck
    rows_ref[:, 448:576] = ncv
    rows_ref[:, 576:704] = nsk
    rows_ref[:, 704:832] = nsv
    wrows_ref[:, 0:128] = nwk
    wrows_ref[:, 128:256] = nwv
    cmp_ref[:, 0:128] = nck
    cmp_ref[:, 128:256] = ncv
    for blk, v in ((KV_KD, dk), (KV_VD, dv), (KV_KI, ik), (KV_KS, nsk), (KV_VS, nsv), (KV_KW, nwk), (KV_VW, nwv)):
        kv_ref[:, blk * LANES:(blk + 1) * LANES] = v.astype(BF16)


def _prep(z, tabs, gains, bd, tm):
    t = z.shape[0]
    nt = tabs[0].shape[0] // tm
    zb = lambda blk: pl.BlockSpec((tm, 512), lambda i: (i, blk))
    tab = pl.BlockSpec((tm, LANES), lambda i: (i % nt, 0))
    full = lambda a: pl.BlockSpec(a.shape, lambda i: (0,) * a.ndim)
    outs = [(CACHE_W, F32), (WIN_W, F32), (2 * KV_W, F32), (N_HEADS * LANES, BF16), (N_HEADS * LANES, BF16),
            (N_HEADS * LANES, BF16), (LANES, F32), (7 * LANES, BF16)]
    return pl.pallas_call(
        _prep_kernel,
        name="prep",
        grid=(t // tm,),
        in_specs=[zb(Z_OFF["dq"] // 512), zb(Z_OFF["iq"] // 512), zb(Z_OFF["nq"] // 512),
                  zb(Z_OFF["dk"] // 512), zb(Z_OFF["nck"] // 512), zb(Z_OFF["nwk"] // 512),
                  tab, tab, tab, full(gains), full(bd)],
        out_specs=[pl.BlockSpec((tm, w), lambda i: (i, 0)) for w, _ in outs],
        out_shape=[jax.ShapeDtypeStruct((t, w), dt) for w, dt in outs],
        compiler_params=_cparams("parallel"),
    )(z, z, z, z, z, z, *tabs, gains, bd)


def _rope_tables(pos):
    half = HEAD_DIM // 8
    inv = ROPE_THETA ** (-jnp.arange(half, dtype=F32) / half)
    ang = pos.astype(F32)[:, None] * inv[None, :]
    cos, sin = jnp.cos(ang), jnp.sin(ang)
    n = pos.shape[0]
    one = jnp.ones((n, HEAD_DIM - 2 * half), F32)
    zero = jnp.zeros((n, HEAD_DIM - 2 * half), F32)
    z8 = jnp.zeros((n, half), F32)
    cosf = jnp.concatenate([cos, cos, one], axis=1)
    sina = jnp.concatenate([-sin, z8, zero], axis=1)
    sinb = jnp.concatenate([z8, sin, zero], axis=1)
    return tuple(jnp.tile(a, (1, 2)) for a in (cosf, sina, sinb))


def _perm_w_in(w_in):
    return (jnp.take(w_in, jnp.asarray(Z_SRC_IDX), axis=1) * jnp.asarray(Z_VALID)[None, :]).astype(BF16)


def _prep_consts(dsa_qk_g, idx_k_g, nsa_qk_g):
    two = lambda g: jnp.tile(g.astype(F32), 2)
    ikg = jnp.concatenate([idx_k_g.astype(F32), jnp.ones((HEAD_DIM,), F32)])
    rows = [two(dsa_qk_g[0]), two(dsa_qk_g[1]), ikg, two(nsa_qk_g[0]), two(nsa_qk_g[1]), two(nsa_qk_g[2]),
            two(nsa_qk_g[3]), jnp.ones((LANES,), F32)]
    bd = np.kron(np.eye(2, dtype=np.float32), np.full((HEAD_DIM, HEAD_DIM), 1.0 / HEAD_DIM, np.float32))
    return jnp.stack(rows), jnp.asarray(bd, BF16)


def _split3(x):
    a = x.astype(BF16)
    r = x - a.astype(F32)
    b = r.astype(BF16)
    c = (r - b.astype(F32)).astype(BF16)
    return a, b, c


def _row_to_col(row, eye):
    return jnp.sum(jnp.where(eye, row, 0.0), axis=1, keepdims=True)


def _hgrn_gates(fr, lb):
    sig = jax.nn.sigmoid(fr)
    f = lb + (1.0 - lb) * sig
    return jnp.log(f), (1.0 - lb) * jax.nn.sigmoid(-fr)


def _hgrn_out(o, gon, gate):
    ms = jnp.mean(o * o, axis=-1, keepdims=True)
    return o * lax.rsqrt(ms + EPS) * gon * (gate * jax.nn.sigmoid(gate))


def _hgrn_prompt_kernel(hq_ref, hf_ref, hi_ref, hg_ref, lb_ref, gon_ref, ya_ref, s_ref, b_scr, k_scr):
    c = HG_CHUNK
    n_chunks = hq_ref.shape[0] // c
    lb, gon = lb_ref[...], gon_ref[...]
    ti = lax.broadcasted_iota(I32, (c, HG_K), 0)
    tril = (lax.broadcasted_iota(I32, (c, c), 0) >= lax.broadcasted_iota(I32, (c, c), 1)).astype(BF16)
    lane_c = lax.broadcasted_iota(I32, (c, c), 1)
    eye = lax.broadcasted_iota(I32, (HG_K, HG_K), 0) == lax.broadcasted_iota(I32, (HG_K, HG_K), 1)
    s_ref[0, 0] = jnp.zeros((HG_K, HG_V), F32)

    def chunk(ci, carry):
        rows = pl.ds(pl.multiple_of(ci * c, c), c)
        g, kk = _hgrn_gates(hf_ref[rows, :], lb)
        hq = hq_ref[rows, :]
        qq = hq * jax.nn.sigmoid(hq)
        v = hi_ref[rows, :]
        b = sum(jnp.dot(tril, part, preferred_element_type=F32) for part in _split3(g))
        b_scr[...] = b
        k_scr[...] = kk
        a_off = {}
        for i in range(1, c // HG_SUB):
            sub = slice(HG_SUB * i, HG_SUB * (i + 1))
            b0 = b[HG_SUB * i - 1:HG_SUB * i, :]
            qh, ql = _split2(qq[sub] * jnp.exp(b[sub] - b0))
            kh, kl = _split2(kk * jnp.exp(jnp.where(ti < HG_SUB * i, b0 - b, -jnp.inf)))
            a_off[i] = _nt(qh, kh) + _nt(qh, kl) + _nt(ql, kh)
        a_tiles = []
        for r0 in range(0, c, 8):
            bt, qt = b[r0:r0 + 8], qq[r0:r0 + 8]
            i, s0 = r0 // HG_SUB, r0 - r0 % HG_SUB
            at = a_off[i][r0 - s0:r0 - s0 + 8] if i else jnp.zeros((8, c), F32)
            for s in range(s0, r0 + 8):
                diff = bt - b_scr[s:s + 1, :]
                if s >= r0:
                    diff = jnp.where(ti[0:8] + r0 >= s, diff, -jnp.inf)
                colv = jnp.sum(qt * jnp.exp(diff) * k_scr[s:s + 1, :], axis=1, keepdims=True)
                at = jnp.where(lane_c[0:8] == s, colv, at)
            a_tiles.append(at)
        a = jnp.concatenate(a_tiles, axis=0)
        s_old = s_ref[0, 0]
        o = (jnp.dot(a.astype(BF16), v.astype(BF16), preferred_element_type=F32)
             + jnp.dot((qq * jnp.exp(b)).astype(BF16), s_old.astype(BF16), preferred_element_type=F32))
        bl = b[c - 1:c, :]
        kdec = (kk * jnp.exp(bl - b)).astype(BF16)
        s_ref[0, 0] = (_row_to_col(jnp.exp(bl), eye) * s_old
                       + lax.dot_general(kdec, v.astype(BF16), (((0,), (0,)), ((), ())),
                                         preferred_element_type=F32))
        ya_ref[rows, :] = _hgrn_out(o, gon, hg_ref[rows, :])
        return carry

    lax.fori_loop(0, n_chunks, chunk, 0, unroll=4 if n_chunks % 4 == 0 else 1)


def _hgrn_prompt(z, lb, gon, batch, seq):
    zb = lambda name: pl.BlockSpec((seq, HG_K), lambda b, h, o=Z_OFF[name] // HG_K: (b, o + h))
    vec = pl.BlockSpec((1, HG_K), lambda b, h: (0, h))
    return pl.pallas_call(
        _hgrn_prompt_kernel,
        name="hgrn_prompt",
        grid=(batch, HG_HEADS),
        in_specs=[zb("hq"), zb("hf"), zb("hi"), zb("hgate"), vec, vec],
        out_specs=[pl.BlockSpec((seq, HG_V), lambda b, h: (b, h)),
                   pl.BlockSpec((1, 1, HG_K, HG_V), lambda b, h: (b, h, 0, 0))],
        out_shape=[jax.ShapeDtypeStruct((batch * seq, BRANCH_W), F32),
                   jax.ShapeDtypeStruct((batch, HG_HEADS, HG_K, HG_V), F32)],
        scratch_shapes=[pltpu.VMEM((HG_CHUNK, HG_K), F32), pltpu.VMEM((HG_CHUNK, HG_K), F32)],
        compiler_params=_cparams("parallel", "parallel"),
    )(z, z, z, z, lb.reshape(1, BRANCH_W), gon.reshape(1, BRANCH_W))


def _hgrn_sample_kernel(hq_ref, hf_ref, hi_ref, hg_ref, lb_ref, gon_ref, s0_ref, ya_ref, s_ref):
    n = hq_ref.shape[0]
    lb, gon = lb_ref[...], gon_ref[...]
    eye = lax.broadcasted_iota(I32, (HG_K, HG_K), 0) == lax.broadcasted_iota(I32, (HG_K, HG_K), 1)

    def seq(b, carry):
        row = pl.ds(b, 1)
        g, kk = _hgrn_gates(hf_ref[row, :], lb)
        hq = hq_ref[row, :]
        qq = hq * jax.nn.sigmoid(hq)
        s_new = (_row_to_col(jnp.exp(g), eye) * s0_ref[b, 0] + _row_to_col(kk, eye) * hi_ref[row, :])
        s_ref[b, 0] = s_new
        o = jnp.sum(_row_to_col(qq, eye) * s_new, axis=0, keepdims=True)
        ya_ref[row, :] = _hgrn_out(o, gon, hg_ref[row, :])
        return carry

    lax.fori_loop(0, n, seq, 0)


def _hgrn_sample(z, lb, gon, s0):
    n = z.shape[0]
    zb = lambda name: pl.BlockSpec((n, HG_K), lambda h, o=Z_OFF[name] // HG_K: (0, o + h))
    vec = pl.BlockSpec((1, HG_K), lambda h: (0, h))
    st = pl.BlockSpec((n, 1, HG_K, HG_V), lambda h: (0, h, 0, 0))
    return pl.pallas_call(
        _hgrn_sample_kernel,
        name="hgrn_sample",
        grid=(HG_HEADS,),
        in_specs=[zb("hq"), zb("hf"), zb("hi"), zb("hgate"), vec, vec, st],
        out_specs=[pl.BlockSpec((n, HG_V), lambda h: (0, h)), st],
        out_shape=[jax.ShapeDtypeStruct((n, BRANCH_W), F32),
                   jax.ShapeDtypeStruct((n, HG_HEADS, HG_K, HG_V), F32)],
        compiler_params=_cparams("parallel"),
    )(z, z, z, z, lb.reshape(1, BRANCH_W), gon.reshape(1, BRANCH_W), s0)


def _compress_consts(pe, w1, b1, w2):
    eye2 = jnp.eye(KV_HEADS, dtype=F32)
    w1r = w1.reshape(2, CMP_BLOCK, HEAD_DIM, HEAD_DIM)
    kron = lambda a: jnp.einsum('ij,ktdh->ktidjh', eye2, a).reshape(2, CMP_STRIDE, KV_W, KV_W)
    w1ab = jnp.concatenate([kron(w1r[:, :CMP_STRIDE]), kron(w1r[:, CMP_STRIDE:])], axis=-1).astype(BF16)
    pe8 = jnp.zeros((2, 8, CMP_BLOCK * HEAD_DIM), F32).at[:, 0].set(pe.reshape(2, -1)).astype(BF16)
    w1d = jnp.tile(w1, (1, 1, KV_HEADS)).astype(BF16)
    b1d = jnp.tile(b1, (1, KV_HEADS)).reshape(2, 1, KV_W).astype(F32)
    w2bd = jnp.einsum('ij,kdh->kidjh', eye2, w2).reshape(2, KV_W, KV_W).astype(BF16)
    return w1ab, pe8, w1d, b1d, w2bd


def _compress_body(load_chunk_rows, n_ch, w1ab_ref, pe_ref, w1d_ref, b1_ref, w2_ref, pb_scr, j):
    acc = jnp.zeros((n_ch, 2 * KV_W), F32)
    for t in range(CMP_STRIDE):
        acc = acc + load_chunk_rows(j, t, w1ab_ref)
    c0 = b1_ref[j] + jnp.dot(pe_ref[j], w1d_ref[j], preferred_element_type=F32)[0:1, :]
    pb_scr[0:n_ch, :] = acc[:, KV_W:]
    pb_scr[n_ch:n_ch + 8, :] = jnp.zeros((8, KV_W), F32)
    hid = acc[:, :KV_W] + pb_scr[pl.ds(1, n_ch), :] + c0
    hid = hid * jax.nn.sigmoid(hid)
    return jnp.dot(hid.astype(BF16), w2_ref[j], preferred_element_type=F32)


def _compress_kernel(ksrc_ref, vsrc_ref, w1ab_ref, pe_ref, w1d_ref, b1_ref, w2_ref, ck_ref, cv_ref, pb_scr):
    n_ch = ksrc_ref.shape[0] // CMP_STRIDE

    def load(j, t, w_ref):
        x = (ksrc_ref, vsrc_ref)[j][pl.ds(t, n_ch, stride=CMP_STRIDE), :]
        return jnp.dot(x.astype(BF16), w_ref[j, t], preferred_element_type=F32)

    for j, out_ref in enumerate((ck_ref, cv_ref)):
        out_ref[0] = _compress_body(load, n_ch, w1ab_ref, pe_ref, w1d_ref, b1_ref, w2_ref, pb_scr, j).astype(BF16)


def _compress(cmp32, consts, batch, seq):
    n_ch = seq // CMP_STRIDE
    full = lambda a: pl.BlockSpec(a.shape, lambda b: (0,) * a.ndim)
    out = pl.BlockSpec((1, n_ch, KV_W), lambda b: (b, 0, 0))
    return pl.pallas_call(
        _compress_kernel,
        name="compress",
        grid=(batch,),
        in_specs=[pl.BlockSpec((seq, KV_W), lambda b: (b, 0)), pl.BlockSpec((seq, KV_W), lambda b: (b, 1))]
        + [full(a) for a in consts],
        out_specs=[out, out],
        out_shape=[jax.ShapeDtypeStruct((batch, n_ch, KV_W), BF16)] * 2,
        scratch_shapes=[pltpu.VMEM((n_ch + 8, KV_W), F32)],
        compiler_params=_cparams("parallel"),
    )(cmp32, cmp32, *consts)


def _cover(n_rows, n_cmp, n_sel, n_cols):
    c_start = np.arange(n_rows)[:, None] * CMP_STRIDE
    s_start = np.arange(n_cols)[None, :] * SEL_BLOCK
    ov = np.clip(np.minimum(c_start + CMP_BLOCK, s_start + SEL_BLOCK) - np.maximum(c_start, s_start), 0, None)
    ov = ov.astype(np.float32) / CMP_BLOCK
    ov[n_cmp:, :] = 0.0
    ov[:, n_sel:] = 0.0
    return jnp.asarray(ov, BF16)


def _kth_largest(count_ge, max_lt, lo, hi, n_valid, k):
    top = hi
    top_ok = count_ge(top) >= k

    def halve(_, c):
        lo, hi = c
        mid = 0.5 * lo + 0.5 * hi
        ge = count_ge(mid) >= k
        return jnp.where(ge, mid, lo), jnp.where(ge, hi, mid)

    def check(lo, hi):
        t = max_lt(hi)
        done = (count_ge(t) >= k) | top_ok | (n_valid < k)
        return t, jnp.where(done, 1.0, 0.0)

    def refine(c):
        it, lo, hi, _, _ = c
        lo, hi = lax.fori_loop(0, 8, halve, (lo, hi))
        return (it + 1, lo, hi) + check(lo, hi)

    lo, hi = lax.fori_loop(0, 24, halve, (lo, hi))
    state = (jnp.int32(0), lo, hi) + check(lo, hi)
    _, _, _, t, _ = lax.while_loop(lambda c: (c[0] < 40) & (jnp.min(c[4]) < 0.5), refine, state)
    return jnp.where(n_valid < k, -jnp.inf, jnp.where(top_ok, top, t))


def _split2(x):
    hi = x.astype(BF16)
    return hi, (x - hi.astype(F32)).astype(BF16)


def _nt(a, b):
    return lax.dot_general(a, b, (((1,), (1,)), ((), ())), preferred_element_type=F32)


def _group_queries(q_ref, g):
    return jnp.concatenate([q_ref[:, h * LANES:(h + 1) * LANES] for h in range(g * GROUP, (g + 1) * GROUP)], axis=0)


def _chain_init(st_ref, c):
    st_ref[c, 0] = jnp.full((GROUP * QB, LANES), NEG, F32)
    st_ref[c, 1] = jnp.zeros((GROUP * QB, LANES), F32)
    st_ref[c, 2] = jnp.zeros((GROUP * QB, LANES), F32)


def _chain_step(st_ref, c, q4, k2, v2, bias2, ones):
    tile = lambda b: jnp.concatenate([b] * GROUP, axis=0)
    s = [_nt(q4, k2[j * QB:(j + 1) * QB]) + tile(bias2[j]) for j in range(2)]
    m_old = st_ref[c, 0]
    m_new = jnp.maximum(m_old, jnp.max(jnp.maximum(s[0], s[1]), axis=1, keepdims=True))
    alpha = jnp.exp(m_old - m_new)
    p = jnp.concatenate([jnp.exp(sj - m_new).astype(BF16) for sj in s], axis=1)
    st_ref[c, 0] = m_new
    st_ref[c, 1] = alpha * st_ref[c, 1] + jnp.dot(p, jnp.concatenate([ones, ones], axis=0),
                                                  preferred_element_type=F32)
    st_ref[c, 2] = alpha * st_ref[c, 2] + jnp.dot(p, v2, preferred_element_type=F32)


def _chain_result(st_ref, c, hh):
    rows = slice(hh * QB, (hh + 1) * QB)
    return st_ref[c, 2, rows, :] / st_ref[c, 1, rows, :]


def _pair_to_lanes(o_even, o_odd, h_even, lane):
    left = o_even if h_even // GROUP == 0 else pltpu.roll(o_even, HEAD_DIM, 1)
    right = o_odd if (h_even + 1) // GROUP == 1 else pltpu.roll(o_odd, HEAD_DIM, 1)
    return jnp.where(lane < HEAD_DIM, left, right)


def _topn_mask(v, n_cand, n_top, lane):
    rank = jnp.zeros(v.shape, F32)
    for i in range(n_cand):
        ci = v[:, i:i + 1]
        rank = rank + jnp.where((ci > v) | ((ci == v) & (lane > i)), 1.0, 0.0)
    return jnp.where((rank < n_top) & (lane < n_cand), 1.0, 0.0)


def _attn_prompt_kernel(qd_ref, qi_ref, qn_ref, iw_ref, nbg_ref,
                        kd_ref, vd_ref, ki_ref, ks_ref, vs_ref, kw_ref, vw_ref, ck_ref, cv_ref, cover_ref,
                        yb_ref, yc_ref, key_scr, db_scr, sb_scr, acc_scr, st_scr, *, n_top, n_sel):
    i = pl.program_id(1)
    nkb = i + 1
    row = lax.broadcasted_iota(I32, (QB, LANES), 0)
    lane = lax.broadcasted_iota(I32, (QB, LANES), 1)
    qpos = i * QB + row

    for h in range(IDX_HEADS):
        acc_scr[h] = jnp.broadcast_to(iw_ref[:, h:h + 1], (QB, LANES))

    def idx_blk(kb, c):
        kblk = ki_ref[pl.ds(pl.multiple_of(kb * QB, QB), QB), :]
        score = jnp.zeros((QB, LANES), F32)
        for h in range(IDX_HEADS):
            score = score + acc_scr[h] * jnp.maximum(_nt(qi_ref[:, h * LANES:(h + 1) * LANES], kblk), 0.0)
        key_scr[kb] = jnp.where(kb * QB + lane <= qpos, score, -jnp.inf)
        return c

    lax.fori_loop(0, nkb, idx_blk, 0)

    def sweep(fn, init, reduce):
        acc = lax.fori_loop(0, nkb, lambda kb, a: fn(a, key_scr[kb]), jnp.full((QB, LANES), init, F32))
        return reduce(acc, axis=1, keepdims=True)

    count_ge = lambda x: sweep(lambda a, k: a + jnp.where(k >= x, 1.0, 0.0), 0.0, jnp.sum)
    max_lt = lambda x: sweep(lambda a, k: jnp.maximum(a, jnp.where(k < x, k, -jnp.inf)), -jnp.inf, jnp.max)
    lo = sweep(lambda a, k: jnp.minimum(a, jnp.where(k == -jnp.inf, jnp.inf, k)), jnp.inf, jnp.min)
    hi = sweep(jnp.maximum, -jnp.inf, jnp.max)
    n_valid = (qpos[:, 0:1] + 1).astype(F32)
    thr = _kth_largest(count_ge, max_lt, lo, hi, n_valid, n_top)
    need = n_top - sweep(lambda a, k: a + jnp.where(k > thr, 1.0, 0.0), 0.0, jnp.sum)
    upper = jnp.where(row <= lane, 1.0, 0.0).astype(BF16)

    def mask_blk(kb, before):
        key = key_scr[kb]
        eq = key == thr
        pref = jnp.dot(jnp.where(eq, 1.0, 0.0).astype(BF16), upper, preferred_element_type=F32) + before
        sel = ((key > thr) | (eq & (pref <= need))) & (kb * QB + lane <= qpos)
        db_scr[kb] = jnp.where(sel, 0.0, NEG)
        return before + jnp.sum(jnp.where(eq, 1.0, 0.0), axis=1, keepdims=True)

    lax.fori_loop(0, nkb, mask_blk, jnp.zeros((QB, 1), F32))

    gb = jax.nn.sigmoid(nbg_ref[...])
    gcol = lambda h, r: gb[:, 3 * h + r:3 * h + r + 1]
    ck, cv = ck_ref[0], cv_ref[0]
    n_ch = ck.shape[0]
    cl = lax.broadcasted_iota(I32, (QB, n_ch), 1)
    cvalid = cl * CMP_STRIDE + (CMP_BLOCK - 1) <= i * QB + lax.broadcasted_iota(I32, (QB, n_ch), 0)
    for g in range(KV_HEADS):
        imp = jnp.zeros((QB, n_ch), F32)
        for h in range(g * GROUP, (g + 1) * GROUP):
            s = jnp.where(cvalid, _nt(qn_ref[:, h * LANES:(h + 1) * LANES], ck), NEG)
            e = jnp.exp(s - jnp.max(s, axis=1, keepdims=True))
            p = jnp.where(cvalid, e / jnp.sum(e, axis=1, keepdims=True), 0.0)
            acc_scr[h] = gcol(h, 0) * jnp.dot(p.astype(BF16), cv, preferred_element_type=F32)
            imp = imp + p
        ih, il = _split2(imp)
        impj = (jnp.dot(ih, cover_ref[...], preferred_element_type=F32)
                + jnp.dot(il, cover_ref[...], preferred_element_type=F32))
        tb = qpos >> 6
        forced = (lane == 0) | (lane == tb) | (lane == tb - 1)
        v = jnp.where(lane * SEL_BLOCK <= qpos, jnp.where(forced, FORCE, impj), NEG)
        sel16 = _topn_mask(v, n_sel, min(SEL_TOPN, n_sel), lane).astype(BF16)

        def sb_blk(kb, c, g=g, sel16=sel16):
            expand = jnp.where(row == 2 * kb + (lane >> 6), 1.0, 0.0).astype(BF16)
            tok = jnp.dot(sel16, expand, preferred_element_type=F32)
            sb_scr[g, kb] = jnp.where((tok > 0.5) & (kb * QB + lane <= qpos), 0.0, NEG)
            return c

        lax.fori_loop(0, nkb, sb_blk, 0)

    ones = jnp.ones((LANES, LANES), BF16)
    for c in range(3 * KV_HEADS):
        _chain_init(st_scr, c)

    def pair_dsa_sel(j, c):
        r = pl.ds(pl.multiple_of(j * 2 * QB, 2 * QB), 2 * QB)
        second = 2 * j + 1 < nkb
        masked = lambda ref, idx: (ref[idx + (2 * j,)], jnp.where(second, ref[idx + (2 * j + 1,)], NEG))
        for g in range(KV_HEADS):
            _chain_step(st_scr, g, _group_queries(qd_ref, g), kd_ref[r, :], vd_ref[r, :], masked(db_scr, ()), ones)
            _chain_step(st_scr, KV_HEADS + g, _group_queries(qn_ref, g), ks_ref[r, :], vs_ref[r, :],
                        masked(sb_scr, (g,)), ones)
        return c

    def pair_win(j, c):
        r = pl.ds(pl.multiple_of(j * 2 * QB, 2 * QB), 2 * QB)
        kpos = [(2 * j + d) * QB + lane for d in range(2)]
        bias2 = [jnp.where((kp <= qpos) & (qpos - kp < WINDOW), 0.0, NEG) for kp in kpos]
        for g in range(KV_HEADS):
            _chain_step(st_scr, 2 * KV_HEADS + g, _group_queries(qn_ref, g), kw_ref[r, :], vw_ref[r, :], bias2, ones)
        return c

    n_pairs = (nkb + 1) // 2
    lax.fori_loop(0, n_pairs, pair_dsa_sel, 0)
    lax.fori_loop(jnp.maximum(i - WINDOW // QB, 0) // 2, n_pairs, pair_win, 0)

    for j in range(N_HEADS // 2):
        cols = slice(j * LANES, (j + 1) * LANES)
        o_d, o_c = [], []
        for h in (2 * j, 2 * j + 1):
            g, hh = divmod(h, GROUP)
            o_d.append(_chain_result(st_scr, g, hh))
            o_c.append(acc_scr[h] + gcol(h, 1) * _chain_result(st_scr, KV_HEADS + g, hh)
                       + gcol(h, 2) * _chain_result(st_scr, 2 * KV_HEADS + g, hh))
        yb_ref[:, cols] = _pair_to_lanes(o_d[0], o_d[1], 2 * j, lane)
        yc_ref[:, cols] = _pair_to_lanes(o_c[0], o_c[1], 2 * j, lane)


def _attn_prompt(qd, qi, qn, iw, z, kv16, ck, cv, batch, seq):
    nq = seq // QB
    assert seq % (2 * QB) == 0
    n_ch = seq // CMP_STRIDE
    n_sel = seq // SEL_BLOCK
    n_cmp = (seq - CMP_BLOCK) // CMP_STRIDE + 1
    cover = _cover(n_ch, n_cmp, n_sel, LANES)
    qspec = pl.BlockSpec((QB, N_HEADS * LANES), lambda b, i: (b * nq + i, 0))
    zspec = lambda name, w: pl.BlockSpec((QB, w), lambda b, i, o=Z_OFF[name] // w: (b * nq + i, o))
    kvspec = lambda blk: pl.BlockSpec((seq, LANES), lambda b, i: (b, blk))
    cspec = pl.BlockSpec((1, n_ch, KV_W), lambda b, i: (b, 0, 0))
    yspec = pl.BlockSpec((QB, BRANCH_W), lambda b, i: (b * nq + i, 0))
    kern = functools.partial(_attn_prompt_kernel, n_top=min(DSA_TOPK, seq // 4), n_sel=n_sel)
    return pl.pallas_call(
        kern,
        name="attn_prompt",
        grid=(batch, nq),
        in_specs=[qspec, qspec, qspec, pl.BlockSpec((QB, LANES), lambda b, i: (b * nq + i, 0)),
                  zspec("nbg", LANES),
                  kvspec(KV_KD), kvspec(KV_VD), kvspec(KV_KI), kvspec(KV_KS), kvspec(KV_VS), kvspec(KV_KW),
                  kvspec(KV_VW), cspec, cspec, pl.BlockSpec(cover.shape, lambda b, i: (0, 0))],
        out_specs=[yspec, yspec],
        out_shape=[jax.ShapeDtypeStruct((batch * seq, BRANCH_W), F32)] * 2,
        scratch_shapes=[pltpu.VMEM((nq, QB, LANES), F32), pltpu.VMEM((nq, QB, LANES), F32),
                        pltpu.VMEM((KV_HEADS, nq, QB, LANES), F32), pltpu.VMEM((N_HEADS, QB, LANES), F32),
                        pltpu.VMEM((3 * KV_HEADS, 3, GROUP * QB, LANES), F32)],
        compiler_params=_cparams("parallel", "arbitrary"),
    )(qd, qi, qn, iw, z, kv16, kv16, kv16, kv16, kv16, kv16, kv16, ck, cv, cover)


R_DK, R_DV, R_IK, R_CK, R_CV, R_SK, R_SV = 0, 128, 256, 320, 448, 576, 704
ROW_GROUPS = ((R_IK, R_CK), (R_DK, R_IK), (R_CK, R_SK), (R_SK, CACHE_W))
G_IDX, G_DSA, G_CMP, G_SEL = range(4)


def _key_chunk(n_blocks):
    return QB * max(d for d in range(1, 17) if n_blocks % d == 0)


def _attn_sample_kernel(pt_ref, cache_ref, new_ref, qd_ref, qi_ref, qn_ref, iw_ref, nbg_ref, win_ref,
                        w1ab_ref, pe_ref, w1d_ref, b1_ref, w2_ref, cover_ref,
                        od_ref, oc_ref, buf, cmp_scr, pb_scr, tok_scr, pref_scr, sems, *, layer, past, n_top):
    b = pl.program_id(0)
    n_pages = past // PAGE_SIZE
    nk = past + QB
    kc = _key_chunk(nk // QB)
    n_ch = past // CMP_STRIDE
    n_sel = -(-(past + 1) // SEL_BLOCK)
    n_selp = cover_ref.shape[1]

    def group_copy(g, p):
        r0, r1 = ROW_GROUPS[g]
        cols = pl.ds(pl.multiple_of(p * PAGE_SIZE, PAGE_SIZE), PAGE_SIZE)
        return pltpu.make_async_copy(cache_ref.at[layer, pt_ref[b, p], pl.ds(r0, r1 - r0), :],
                                     buf.at[pl.ds(r0, r1 - r0), cols], sems.at[g])

    def each_page(fn):
        def body(p, c):
            fn(p)
            return c
        lax.fori_loop(0, n_pages, body, 0)

    for g in range(len(ROW_GROUPS)):
        each_page(lambda p, g=g: group_copy(g, p).start())
    buf[:, pl.ds(past, QB)] = jnp.zeros((CACHE_W, QB), F32)
    buf[:, pl.ds(past, 1)] = new_ref[0]
    arrived = lambda g: each_page(lambda p: group_copy(g, p).wait())

    kpos1 = lax.broadcasted_iota(I32, (1, nk), 1)
    in_range = kpos1 <= past
    lo_heads = lax.broadcasted_iota(I32, (N_HEADS, HEAD_DIM), 0) < GROUP

    def scores(q, r0):
        n = q.shape[1]
        return jnp.concatenate([jnp.dot(q, buf[r0:r0 + n, c:c + kc], preferred_element_type=F32)
                                for c in range(0, nk, kc)], axis=1)

    def weighted(p, r0):
        return sum(_nt(p[:, c:c + kc], buf[r0:r0 + KV_W, c:c + kc]) for c in range(0, nk, kc))

    def softmax(s):
        e = jnp.exp(s - jnp.max(s, axis=1, keepdims=True))
        return e / jnp.sum(e, axis=1, keepdims=True)

    compact = lambda o: jnp.where(lo_heads, o[:, :HEAD_DIM], o[:, HEAD_DIM:])

    arrived(G_IDX)
    sc = jnp.sum(iw_ref[0] * jnp.maximum(scores(qi_ref[0], R_IK), 0.0), axis=0, keepdims=True)
    key = jnp.where(in_range, sc, -jnp.inf)
    count = lambda pred: jnp.sum(jnp.where(pred, 1.0, 0.0), axis=1, keepdims=True)
    thr = _kth_largest(lambda x: count(key >= x),
                       lambda x: jnp.max(jnp.where(key < x, key, -jnp.inf), axis=1, keepdims=True),
                       jnp.min(jnp.where(in_range, sc, jnp.inf), axis=1, keepdims=True),
                       jnp.max(key, axis=1, keepdims=True), jnp.full((1, 1), past + 1.0, F32), n_top)
    need = n_top - count(key > thr)
    eq8 = jnp.where(jnp.broadcast_to(key == thr, (N_HEADS, nk)), 1.0, 0.0)
    upper = jnp.where(lax.broadcasted_iota(I32, (LANES, LANES), 0) <= lax.broadcasted_iota(I32, (LANES, LANES), 1),
                      1.0, 0.0).astype(BF16)
    before = jnp.zeros((N_HEADS, 1), F32)
    for c in range(0, nk, LANES):
        pref = jnp.dot(eq8[:, c:c + LANES].astype(BF16), upper, preferred_element_type=F32) + before
        pref_scr[:, c:c + LANES] = pref
        before = before + jnp.sum(eq8[:, c:c + LANES], axis=1, keepdims=True)
    sel = ((key > thr) | ((key == thr) & (pref_scr[0:1, :] <= need))) & in_range
    arrived(G_DSA)
    p = softmax(scores(qd_ref[0], R_DK) + jnp.where(sel, 0.0, NEG))
    od_ref[0] = compact(weighted(p, R_DV))

    arrived(G_CMP)
    per_page = PAGE_SIZE // CMP_STRIDE
    pr = lax.broadcasted_iota(I32, (PAGE_SIZE, PAGE_SIZE), 0)
    pick = jnp.where((pr % per_page) * CMP_STRIDE + pr // per_page == lax.broadcasted_iota(I32, pr.shape, 1),
                     1.0, 0.0).astype(BF16)

    def regroup(p):
        cols = pl.ds(pl.multiple_of(p * PAGE_SIZE, PAGE_SIZE), PAGE_SIZE)
        chunks = pl.ds(pl.multiple_of(p * per_page, per_page), per_page)
        for j, r0 in enumerate((R_CK, R_CV)):
            xt = _nt(pick, buf[r0:r0 + KV_W, cols].astype(BF16))
            for t in range(CMP_STRIDE):
                cmp_scr[j, t, chunks, :] = xt[t * per_page:(t + 1) * per_page]

    each_page(regroup)

    def load(j, t, w_ref):
        return jnp.dot(cmp_scr[j, t].astype(BF16), w_ref[j, t], preferred_element_type=F32)

    ck = _compress_body(load, n_ch, w1ab_ref, pe_ref, w1d_ref, b1_ref, w2_ref, pb_scr, 0).astype(BF16)
    cv = _compress_body(load, n_ch, w1ab_ref, pe_ref, w1d_ref, b1_ref, w2_ref, pb_scr, 1).astype(BF16)

    gb = jax.nn.sigmoid(nbg_ref[0])
    qn = qn_ref[0]
    cvalid = lax.broadcasted_iota(I32, (N_HEADS, n_ch), 1) * CMP_STRIDE + (CMP_BLOCK - 1) <= past
    s = jnp.where(cvalid, _nt(qn.astype(BF16), ck), NEG)
    e = jnp.exp(s - jnp.max(s, axis=1, keepdims=True))
    pc = jnp.where(cvalid, e / jnp.sum(e, axis=1, keepdims=True), 0.0)
    o_cmp = compact(jnp.dot(pc.astype(BF16), cv, preferred_element_type=F32))
    imp = jnp.where(lax.broadcasted_iota(I32, (N_HEADS, n_ch), 0) < GROUP,
                    jnp.sum(pc[:GROUP], axis=0, keepdims=True), jnp.sum(pc[GROUP:], axis=0, keepdims=True))
    ih, il = _split2(imp)
    impj = (jnp.dot(ih, cover_ref[...], preferred_element_type=F32)
            + jnp.dot(il, cover_ref[...], preferred_element_type=F32))
    lane_s = lax.broadcasted_iota(I32, (N_HEADS, n_selp), 1)
    tb = past // SEL_BLOCK
    forced = (lane_s == 0) | (lane_s == tb) | (lane_s == tb - 1)
    v = jnp.where(lane_s * SEL_BLOCK <= past, jnp.where(forced, FORCE, impj), NEG)
    sel8 = _topn_mask(v, n_sel, min(SEL_TOPN, n_sel), lane_s)
    half = lax.broadcasted_iota(I32, (N_HEADS, LANES), 1) < SEL_BLOCK
    for c in range(nk // LANES):
        tok_scr[:, c * LANES:(c + 1) * LANES] = jnp.where(half, sel8[:, 2 * c:2 * c + 1], sel8[:, 2 * c + 1:2 * c + 2])

    arrived(G_SEL)
    p = softmax(jnp.where((tok_scr[...] > 0.5) & in_range, scores(qn, R_SK), NEG))
    o_sel = compact(weighted(p, R_SV))

    pw = softmax(_nt(qn, win_ref[0, :, 0:KV_W]))
    o_win = compact(jnp.dot(pw, win_ref[0, :, KV_W:2 * KV_W], preferred_element_type=F32))
    oc_ref[0] = gb[:, 0:1] * o_cmp + gb[:, 1:2] * o_sel + gb[:, 2:3] * o_win


def _attn_sample(page_table, cache_t, layer, new_rows, qd, qi, qn, iw, nbg, win, cmp_consts):
    nb, n_pages = page_table.shape
    past = n_pages * PAGE_SIZE
    nk = past + QB
    n_ch = past // CMP_STRIDE
    n_sel = -(-(past + 1) // SEL_BLOCK)
    n_selp = -(-n_sel // LANES) * LANES
    n_cmp = (past + 1 - CMP_BLOCK) // CMP_STRIDE + 1
    cover = _cover(n_ch, n_cmp, n_sel, n_selp)
    consts = tuple(cmp_consts) + (cover,)
    per_seq = lambda a: pl.BlockSpec((1,) + a.shape[1:], lambda b, pt: (b,) + (0,) * (a.ndim - 1))
    full = lambda a: pl.BlockSpec(a.shape, lambda b, pt: (0,) * a.ndim)
    args = (new_rows, qd, qi, qn, iw, nbg, win)
    kern = functools.partial(_attn_sample_kernel, layer=layer, past=past, n_top=min(DSA_TOPK, (past + 1) // 4))
    out = pl.BlockSpec((1, N_HEADS, HEAD_DIM), lambda b, pt: (b, 0, 0))
    return pl.pallas_call(
        kern,
        name="attn_sample",
        grid_spec=pltpu.PrefetchScalarGridSpec(
            num_scalar_prefetch=1,
            grid=(nb,),
            in_specs=[pl.BlockSpec(memory_space=pl.ANY)] + [per_seq(a) for a in args] + [full(a) for a in consts],
            out_specs=[out, out],
            scratch_shapes=[pltpu.VMEM((CACHE_W, nk), F32), pltpu.VMEM((2, CMP_STRIDE, n_ch, KV_W), F32),
                            pltpu.VMEM((n_ch + 8, KV_W), F32),
                            pltpu.VMEM((N_HEADS, nk), F32), pltpu.VMEM((N_HEADS, nk), F32),
                            pltpu.SemaphoreType.DMA((len(ROW_GROUPS),))]),
        out_shape=[jax.ShapeDtypeStruct((nb, N_HEADS, HEAD_DIM), F32)] * 2,
        compiler_params=_cparams("arbitrary"),
    )(page_table, cache_t, *args, *consts)


def _merge_kernel(x_ref, ya_ref, yb_ref, yc_ref, dg_ref, ng_ref, ga_ref, gb_ref, gc_ref, wb_ref, wo_ref, y_ref):
    silu = lambda v: v * jax.nn.sigmoid(v)
    ys = (ya_ref[...], yb_ref[...] * silu(dg_ref[...]), yc_ref[...] * silu(ng_ref[...]))
    m = None
    for n, (y, g) in enumerate(zip(ys, (ga_ref, gb_ref, gc_ref))):
        pr = jax.nn.sigmoid(g[...]) * jnp.dot(y.astype(BF16), wb_ref[n], preferred_element_type=F32)
        m = pr if m is None else m + pr
    y_ref[...] = x_ref[...] + jnp.dot(m.astype(BF16), wo_ref[...], preferred_element_type=F32)


def _merge(x2, ya, yb, yc, z, wb16, wo16):
    t, d = x2.shape
    tm = min(t, 512)
    yspec = pl.BlockSpec((tm, BRANCH_W), lambda i: (i, 0))
    gspec = lambda name, w=d: pl.BlockSpec((tm, w), lambda i, o=Z_OFF[name] // w: (i, o))
    return pl.pallas_call(
        _merge_kernel,
        name="merge",
        grid=(t // tm,),
        in_specs=[pl.BlockSpec((tm, d), lambda i: (i, 0)), yspec, yspec, yspec,
                  gspec("dgate", BRANCH_W), gspec("ngate", BRANCH_W), gspec("ma"), gspec("mb"), gspec("mc"),
                  pl.BlockSpec(wb16.shape, lambda i: (0, 0, 0)), pl.BlockSpec(wo16.shape, lambda i: (0, 0))],
        out_specs=pl.BlockSpec((tm, d), lambda i: (i, 0)),
        out_shape=jax.ShapeDtypeStruct((t, d), F32),
        compiler_params=_cparams("parallel"),
    )(x2, ya, yb, yc, z, z, z, z, z, wb16, wo16)


def _layer_prompt(x, lb, norm_g, w16, gon, prep_consts, cmp_consts, wb16, wo16):
    batch, seq, d = x.shape
    x2 = x.reshape(batch * seq, d)
    z = _proj(x2, norm_g, w16)
    tabs = _rope_tables(jnp.arange(seq, dtype=I32))
    rows, wrows, cmp32, qd, qi, qn, iw, kv16 = _prep(z, tabs, *prep_consts, min(seq, 256))
    ya, s_new = _hgrn_prompt(z, lb, gon, batch, seq)
    ck, cv = _compress(cmp32, cmp_consts, batch, seq)
    yb, yc = _attn_prompt(qd, qi, qn, iw, z, kv16, ck, cv, batch, seq)
    y = _merge(x2, ya, yb, yc, z, wb16, wo16).reshape(batch, seq, d)
    wrows = wrows.reshape(batch, seq, WIN_W)
    return y, rows.reshape(batch, seq, CACHE_W), wrows[:, -min(WINDOW, seq):], s_new


def _layer_sample(x, lb, norm_g, w16, gon, prep_consts, cmp_consts, wb16, wo16,
                  page_table, cache_kv, layer, win_rows, s0):
    nb, one, d = x.shape
    assert one == 1 and win_rows.shape[1] == WINDOW
    past = page_table.shape[1] * PAGE_SIZE
    x2 = x.reshape(nb, d)
    z = _proj(x2, norm_g, w16)
    tabs = _rope_tables(jnp.full((nb,), past, I32))
    rows, wrows, _, qd, qi, qn, iw, _ = _prep(z, tabs, *prep_consts, nb)
    ya, s_new = _hgrn_sample(z, lb, gon, s0)
    win = jnp.concatenate([win_rows[:, 1:], wrows[:, None, :]], axis=1)
    heads = lambda q: q.astype(F32).reshape(nb, N_HEADS, LANES)
    nbg = z[:, Z_OFF["nbg"]:Z_OFF["nbg"] + 3 * N_HEADS].reshape(nb, N_HEADS, 3)
    od, oc = _attn_sample(page_table, cache_kv, layer, rows.reshape(nb, CACHE_W, 1), heads(qd),
                          heads(qi)[:, :, :IDX_DIM], heads(qn), iw[:, :IDX_HEADS].reshape(nb, IDX_HEADS, 1), nbg, win,
                          cmp_consts)
    y = _merge(x2, ya, od.reshape(nb, BRANCH_W), oc.reshape(nb, BRANCH_W), z, wb16, wo16)
    return y.reshape(nb, 1, d), rows.reshape(nb, 1, CACHE_W), win, s_new


def kernel(x_prompt, x_sample, cache_kv, page_table, state_win, state_hgrn, norm_g, w_in,
           hgrn_lb_logits, hgrn_onorm_g, dsa_qk_norm_g, dsa_idx_k_norm_g, nsa_qk_norm_g,
           nsa_cmp_pe, nsa_cmp_w1, nsa_cmp_b1, nsa_cmp_w2, w_branch, w_out):
    lb_soft = jax.nn.softmax(hgrn_lb_logits.astype(F32), axis=0)
    lbs = jnp.cumsum(lb_soft, axis=0) - lb_soft[0]
    xp, xs = x_prompt, x_sample
    cache_t = jnp.swapaxes(cache_kv, 2, 3)
    outs = [[] for _ in range(6)]
    for l in range(w_in.shape[0]):
        shared = (lbs[l], norm_g[l], _perm_w_in(w_in[l]), hgrn_onorm_g[l],
                  _prep_consts(dsa_qk_norm_g[l], dsa_idx_k_norm_g[l], nsa_qk_norm_g[l]),
                  _compress_consts(nsa_cmp_pe[l], nsa_cmp_w1[l], nsa_cmp_b1[l], nsa_cmp_w2[l]),
                  w_branch[l].astype(BF16), w_out[l].astype(BF16))
        xs, *s_out = _layer_sample(xs, *shared, page_table, cache_t, l, state_win[l], state_hgrn[l])
        xp, *p_out = _layer_prompt(xp, *shared)
        for acc, o in zip(outs, p_out + s_out):
            acc.append(o)
    return (xp, xs) + tuple(jnp.stack(o) for o in outs)
```

```python
import functools

import jax
import jax.numpy as jnp
from jax import lax
import numpy as np
from jax.experimental import pallas as pl
from jax.experimental.pallas import tpu as pltpu

F32 = jnp.float32
BF16 = jnp.bfloat16
I32 = jnp.int32

D_MODEL = 1024
PAGE_SIZE = 128
BRANCH_W = D_MODEL // 2
HEAD_DIM = 64
ROPE_THETA = 500000.0
EPS = 1e-6
NEG = -1e30
FORCE = 1e9
HG_HEADS = 4
HG_K = BRANCH_W // HG_HEADS
HG_V = BRANCH_W // HG_HEADS
HG_CHUNK = 64
HG_SUB = 16
N_HEADS = BRANCH_W // HEAD_DIM
KV_HEADS = 2
GROUP = N_HEADS // KV_HEADS
IDX_HEADS = 8
IDX_DIM = 64
DSA_TOPK = 256
CMP_BLOCK = 32
CMP_STRIDE = 16
SEL_BLOCK = 64
SEL_TOPN = 16
WINDOW = 512
KV_W = KV_HEADS * HEAD_DIM
CACHE_W = 2 * KV_W + IDX_DIM + 4 * KV_W
WIN_W = 2 * KV_W
LANES = 128
QB = 128
INT_MIN = np.int32(-2 ** 31)
VMEM_LIMIT = 56 * 1024 * 1024

IN_SPLITS = (
    ("hq", 512), ("hf", 512), ("hi", 512), ("hgate", 512),
    ("dq", 512), ("dk", 128), ("dv", 128), ("iq", 512), ("ik", 64), ("iw", 8), ("dgate", 512),
    ("nq", 512), ("nck", 128), ("ncv", 128), ("nsk", 128), ("nsv", 128), ("nwk", 128), ("nwv", 128),
    ("nbg", 24), ("ngate", 512), ("ma", 1024), ("mb", 1024), ("mc", 1024),
)
Z_ORDER = ("ma", "mb", "mc", "hq", "hf", "hi", "hgate", "dq", "iq", "nq", "dgate", "ngate",
           "dk", "dv", "ik", "iw", "nck", "ncv", "nsk", "nsv", "nwk", "nwv", "nbg")
Z_COLS = 9216


def _z_layout():
    widths = dict(IN_SPLITS)
    src, off = {}, 0
    for name, w in IN_SPLITS:
        src[name] = off
        off += w
    dst, off = {}, 0
    for name in Z_ORDER:
        dst[name] = off
        off += -(-widths[name] // LANES) * LANES
    assert off <= Z_COLS
    idx = np.zeros((Z_COLS,), np.int32)
    valid = np.zeros((Z_COLS,), np.float32)
    for name in Z_ORDER:
        w = widths[name]
        idx[dst[name]:dst[name] + w] = src[name] + np.arange(w)
        valid[dst[name]:dst[name] + w] = 1.0
    return dst, idx, valid


Z_OFF, Z_SRC_IDX, Z_VALID = _z_layout()


def _cparams(*sem):
    return pltpu.CompilerParams(dimension_semantics=sem, vmem_limit_bytes=VMEM_LIMIT)


def _proj_kernel(x_ref, g_ref, w_ref, z_ref, h_scr):
    @pl.when(pl.program_id(1) == 0)
    def _():
        x = x_ref[...]
        ms = jnp.mean(x * x, axis=-1, keepdims=True)
        h_scr[...] = (x * lax.rsqrt(ms + EPS) * g_ref[...]).astype(BF16)

    z_ref[...] = jnp.dot(h_scr[...], w_ref[...], preferred_element_type=F32)


def _proj(x2, g, w16):
    t, d = x2.shape
    tm = min(t, 1024)
    tn = 1536
    assert t % tm == 0
    return pl.pallas_call(
        _proj_kernel,
        name="proj",
        grid=(t // tm, Z_COLS // tn),
        in_specs=[pl.BlockSpec((tm, d), lambda i, j: (i, 0)),
                  pl.BlockSpec((1, d), lambda i, j: (0, 0)),
                  pl.BlockSpec((d, tn), lambda i, j: (0, j))],
        out_specs=pl.BlockSpec((tm, tn), lambda i, j: (i, j)),
        out_shape=jax.ShapeDtypeStruct((t, Z_COLS), F32),
        scratch_shapes=[pltpu.VMEM((tm, d), BF16)],
        compiler_params=_cparams("parallel", "arbitrary"),
    )(x2, g.reshape(1, d), w16)


KV_KD, KV_VD, KV_KI, KV_KS, KV_VS, KV_KW, KV_VW = range(7)
G_DQ, G_DK, G_IK, G_NQ, G_NCK, G_NSK, G_NWK = range(7)


def _head_rms(x, gain, bd):
    x2 = x * x
    hi = x2.astype(BF16)
    lo = (x2 - hi.astype(F32)).astype(BF16)
    ms = jnp.dot(hi, bd, preferred_element_type=F32) + jnp.dot(lo, bd, preferred_element_type=F32)
    return x * lax.rsqrt(ms + EPS) * gain


def _rope128(x, cosf, sina, sinb):
    return x * cosf + pltpu.roll(x, LANES - 8, 1) * sina + pltpu.roll(x, 8, 1) * sinb


def _prep_kernel(dq_ref, iq_ref, nq_ref, s0_ref, s1_ref, s2_ref, cos_ref, sina_ref, sinb_ref, g_ref, bd_ref,
                 rows_ref, wrows_ref, cmp_ref, qd_ref, qi_ref, qn_ref, iw_ref, kv_ref):
    cosf, sina, sinb = cos_ref[...], sina_ref[...], sinb_ref[...]
    bd = bd_ref[...]
    lane = lax.broadcasted_iota(I32, cosf.shape, 1)
    lo_half = lane < HEAD_DIM
    rope = lambda v: _rope128(v, cosf, sina, sinb)
    gain = lambda r: g_ref[r:r + 1, :]

    def put_q(src_ref, out_ref, g_row, scale, target_of_head):
        for j in range(N_HEADS // 2):
            x = src_ref[:, j * LANES:(j + 1) * LANES]
            if g_row is not None:
                x = _head_rms(x, gain(g_row), bd)
            x = rope(x) * scale
            xr = pltpu.roll(x, HEAD_DIM, 1)
            for half in range(2):
                h = 2 * j + half
                tgt = target_of_head(h)
                v = x if tgt == half else xr
                keep = lo_half if tgt == 0 else jnp.logical_not(lo_half)
                out_ref[:, h * LANES:(h + 1) * LANES] = jnp.where(keep, v, 0.0).astype(BF16)

    put_q(dq_ref, qd_ref, G_DQ, HEAD_DIM ** -0.5, lambda h: h // GROUP)
    put_q(iq_ref, qi_ref, None, 1.0, lambda h: 0)
    put_q(nq_ref, qn_ref, G_NQ, HEAD_DIM ** -0.5, lambda h: h // GROUP)

    dk = rope(_head_rms(s0_ref[:, 0:128], gain(G_DK), bd))
    dv = s0_ref[:, 128:256]
    ik = rope(_head_rms(s0_ref[:, 256:384], gain(G_IK), bd))
    iw_ref[...] = s0_ref[:, 384:512] * (IDX_HEADS ** -0.5 * IDX_DIM ** -0.5)
    nck = rope(_head_rms(s1_ref[:, 0:128], gain(G_NCK), bd))
    ncv = s1_ref[:, 128:256]
    nsk = rope(_head_rms(s1_ref[:, 256:384], gain(G_NSK), bd))
    nsv = s1_ref[:, 384:512]
    nwk = rope(_head_rms(s2_ref[:, 0:128], gain(G_NWK), bd))
    nwv = s2_ref[:, 128:256]

    rows_ref[:, 0:128] = dk
    rows_ref[:, 128:256] = dv
    rows_ref[:, 256:320] = ik[:, 0:64]
    rows_ref[:, 320:448] = nck
    rows_ref[:, 448:576] = ncv
    rows_ref[:, 576:704] = nsk
    rows_ref[:, 704:832] = nsv
    wrows_ref[:, 0:128] = nwk
    wrows_ref[:, 128:256] = nwv
    cmp_ref[:, 0:128] = nck
    cmp_ref[:, 128:256] = ncv
    for blk, v in ((KV_KD, dk), (KV_VD, dv), (KV_KI, ik), (KV_KS, nsk), (KV_VS, nsv), (KV_KW, nwk), (KV_VW, nwv)):
        kv_ref[:, blk * LANES:(blk + 1) * LANES] = v.astype(BF16)


def _prep(z, tabs, gains, bd, tm):
    t = z.shape[0]
    nt = tabs[0].shape[0] // tm
    zb = lambda blk: pl.BlockSpec((tm, 512), lambda i: (i, blk))
    tab = pl.BlockSpec((tm, LANES), lambda i: (i % nt, 0))
    full = lambda a: pl.BlockSpec(a.shape, lambda i: (0,) * a.ndim)
    outs = [(CACHE_W, F32), (WIN_W, F32), (2 * KV_W, F32), (N_HEADS * LANES, BF16), (N_HEADS * LANES, BF16),
            (N_HEADS * LANES, BF16), (LANES, F32), (7 * LANES, BF16)]
    return pl.pallas_call(
        _prep_kernel,
        name="prep",
        grid=(t // tm,),
        in_specs=[zb(Z_OFF["dq"] // 512), zb(Z_OFF["iq"] // 512), zb(Z_OFF["nq"] // 512),
                  zb(Z_OFF["dk"] // 512), zb(Z_OFF["nck"] // 512), zb(Z_OFF["nwk"] // 512),
                  tab, tab, tab, full(gains), full(bd)],
        out_specs=[pl.BlockSpec((tm, w), lambda i: (i, 0)) for w, _ in outs],
        out_shape=[jax.ShapeDtypeStruct((t, w), dt) for w, dt in outs],
        compiler_params=_cparams("parallel"),
    )(z, z, z, z, z, z, *tabs, gains, bd)


def _rope_tables(pos):
    half = HEAD_DIM // 8
    inv = ROPE_THETA ** (-jnp.arange(half, dtype=F32) / half)
    ang = pos.astype(F32)[:, None] * inv[None, :]
    cos, sin = jnp.cos(ang), jnp.sin(ang)
    n = pos.shape[0]
    one = jnp.ones((n, HEAD_DIM - 2 * half), F32)
    zero = jnp.zeros((n, HEAD_DIM - 2 * half), F32)
    z8 = jnp.zeros((n, half), F32)
    cosf = jnp.concatenate([cos, cos, one], axis=1)
    sina = jnp.concatenate([-sin, z8, zero], axis=1)
    sinb = jnp.concatenate([z8, sin, zero], axis=1)
    return tuple(jnp.tile(a, (1, 2)) for a in (cosf, sina, sinb))


def _perm_w_in(w_in):
    widths = dict(IN_SPLITS)
    src, off = {}, 0
    for name, w in IN_SPLITS:
        src[name] = off
        off += w
    parts, pos = [], 0
    for name in Z_ORDER:
        w = widths[name]
        parts.append(w_in[:, src[name]:src[name] + w].astype(BF16))
        pos += w
        pad = -pos % LANES
        if pad:
            parts.append(jnp.zeros((w_in.shape[0], pad), BF16))
            pos += pad
    parts.append(jnp.zeros((w_in.shape[0], Z_COLS - pos), BF16))
    return jnp.concatenate(parts, axis=1)


def _prep_consts(dsa_qk_g, idx_k_g, nsa_qk_g):
    two = lambda g: jnp.tile(g.astype(F32), 2)
    ikg = jnp.concatenate([idx_k_g.astype(F32), jnp.ones((HEAD_DIM,), F32)])
    rows = [two(dsa_qk_g[0]), two(dsa_qk_g[1]), ikg, two(nsa_qk_g[0]), two(nsa_qk_g[1]), two(nsa_qk_g[2]),
            two(nsa_qk_g[3]), jnp.ones((LANES,), F32)]
    bd = np.kron(np.eye(2, dtype=np.float32), np.full((HEAD_DIM, HEAD_DIM), 1.0 / HEAD_DIM, np.float32))
    return jnp.stack(rows), jnp.asarray(bd, BF16)


def _split3(x):
    a = x.astype(BF16)
    r = x - a.astype(F32)
    b = r.astype(BF16)
    c = (r - b.astype(F32)).astype(BF16)
    return a, b, c


def _row_to_col(row, eye):
    return jnp.sum(jnp.where(eye, row, 0.0), axis=1, keepdims=True)


def _hgrn_gates(fr, lb):
    sig = jax.nn.sigmoid(fr)
    f = lb + (1.0 - lb) * sig
    return jnp.log(f), (1.0 - lb) * jax.nn.sigmoid(-fr)


def _hgrn_out(o, gon, gate):
    ms = jnp.mean(o * o, axis=-1, keepdims=True)
    return o * lax.rsqrt(ms + EPS) * gon * (gate * jax.nn.sigmoid(gate))


def _hgrn_prompt_kernel(hq_ref, hf_ref, hi_ref, hg_ref, lb_ref, gon_ref, ya_ref, s_ref, b_scr, k_scr):
    c = HG_CHUNK
    n_chunks = hq_ref.shape[0] // c
    lb, gon = lb_ref[...], gon_ref[...]
    ti = lax.broadcasted_iota(I32, (c, HG_K), 0)
    tril = (lax.broadcasted_iota(I32, (c, c), 0) >= lax.broadcasted_iota(I32, (c, c), 1)).astype(BF16)
    lane_c = lax.broadcasted_iota(I32, (c, c), 1)
    eye = lax.broadcasted_iota(I32, (HG_K, HG_K), 0) == lax.broadcasted_iota(I32, (HG_K, HG_K), 1)
    s_ref[0, 0] = jnp.zeros((HG_K, HG_V), F32)

    def chunk(ci, carry):
        rows = pl.ds(pl.multiple_of(ci * c, c), c)
        g, kk = _hgrn_gates(hf_ref[rows, :], lb)
        hq = hq_ref[rows, :]
        qq = hq * jax.nn.sigmoid(hq)
        v = hi_ref[rows, :]
        b = sum(jnp.dot(tril, part, preferred_element_type=F32) for part in _split3(g))
        b_scr[...] = b
        k_scr[...] = kk
        a_off = {}
        for i in range(1, c // HG_SUB):
            sub = slice(HG_SUB * i, HG_SUB * (i + 1))
            b0 = b[HG_SUB * i - 1:HG_SUB * i, :]
            qh, ql = _split2(qq[sub] * jnp.exp(b[sub] - b0))
            kh, kl = _split2(kk * jnp.exp(jnp.where(ti < HG_SUB * i, b0 - b, -jnp.inf)))
            a_off[i] = _nt(qh, kh) + _nt(qh, kl) + _nt(ql, kh)
        a_tiles = []
        for r0 in range(0, c, 8):
            bt, qt = b[r0:r0 + 8], qq[r0:r0 + 8]
            i, s0 = r0 // HG_SUB, r0 - r0 % HG_SUB
            at = a_off[i][r0 - s0:r0 - s0 + 8] if i else jnp.zeros((8, c), F32)
            for s in range(s0, r0 + 8):
                diff = bt - b_scr[s:s + 1, :]
                if s >= r0:
                    diff = jnp.where(ti[0:8] + r0 >= s, diff, -jnp.inf)
                colv = jnp.sum(qt * jnp.exp(diff) * k_scr[s:s + 1, :], axis=1, keepdims=True)
                at = jnp.where(lane_c[0:8] == s, colv, at)
            a_tiles.append(at)
        a = jnp.concatenate(a_tiles, axis=0)
        s_old = s_ref[0, 0]
        o = (jnp.dot(a.astype(BF16), v.astype(BF16), preferred_element_type=F32)
             + jnp.dot((qq * jnp.exp(b)).astype(BF16), s_old.astype(BF16), preferred_element_type=F32))
        bl = b[c - 1:c, :]
        kdec = (kk * jnp.exp(bl - b)).astype(BF16)
        s_ref[0, 0] = (_row_to_col(jnp.exp(bl), eye) * s_old
                       + lax.dot_general(kdec, v.astype(BF16), (((0,), (0,)), ((), ())),
                                         preferred_element_type=F32))
        ya_ref[rows, :] = _hgrn_out(o, gon, hg_ref[rows, :])
        return carry

    lax.fori_loop(0, n_chunks, chunk, 0, unroll=8 if n_chunks % 8 == 0 else 1)


def _hgrn_prompt(z, lb, gon, batch, seq):
    zb = lambda name: pl.BlockSpec((seq, HG_K), lambda b, h, o=Z_OFF[name] // HG_K: (b, o + h))
    vec = pl.BlockSpec((1, HG_K), lambda b, h: (0, h))
    return pl.pallas_call(
        _hgrn_prompt_kernel,
        name="hgrn_prompt",
        grid=(batch, HG_HEADS),
        in_specs=[zb("hq"), zb("hf"), zb("hi"), zb("hgate"), vec, vec],
        out_specs=[pl.BlockSpec((seq, HG_V), lambda b, h: (b, h)),
                   pl.BlockSpec((1, 1, HG_K, HG_V), lambda b, h: (b, h, 0, 0))],
        out_shape=[jax.ShapeDtypeStruct((batch * seq, BRANCH_W), F32),
                   jax.ShapeDtypeStruct((batch, HG_HEADS, HG_K, HG_V), F32)],
        scratch_shapes=[pltpu.VMEM((HG_CHUNK, HG_K), F32), pltpu.VMEM((HG_CHUNK, HG_K), F32)],
        compiler_params=_cparams("parallel", "parallel"),
    )(z, z, z, z, lb.reshape(1, BRANCH_W), gon.reshape(1, BRANCH_W))


def _hgrn_sample_kernel(hq_ref, hf_ref, hi_ref, hg_ref, lb_ref, gon_ref, s0_ref, ya_ref, s_ref):
    n = hq_ref.shape[0]
    lb, gon = lb_ref[...], gon_ref[...]
    eye = lax.broadcasted_iota(I32, (HG_K, HG_K), 0) == lax.broadcasted_iota(I32, (HG_K, HG_K), 1)

    def seq(b, carry):
        row = pl.ds(b, 1)
        g, kk = _hgrn_gates(hf_ref[row, :], lb)
        hq = hq_ref[row, :]
        qq = hq * jax.nn.sigmoid(hq)
        s_new = (_row_to_col(jnp.exp(g), eye) * s0_ref[b, 0] + _row_to_col(kk, eye) * hi_ref[row, :])
        s_ref[b, 0] = s_new
        o = jnp.sum(_row_to_col(qq, eye) * s_new, axis=0, keepdims=True)
        ya_ref[row, :] = _hgrn_out(o, gon, hg_ref[row, :])
        return carry

    lax.fori_loop(0, n, seq, 0)


def _hgrn_sample(z, lb, gon, s0):
    n = z.shape[0]
    zb = lambda name: pl.BlockSpec((n, HG_K), lambda h, o=Z_OFF[name] // HG_K: (0, o + h))
    vec = pl.BlockSpec((1, HG_K), lambda h: (0, h))
    st = pl.BlockSpec((n, 1, HG_K, HG_V), lambda h: (0, h, 0, 0))
    return pl.pallas_call(
        _hgrn_sample_kernel,
        name="hgrn_sample",
        grid=(HG_HEADS,),
        in_specs=[zb("hq"), zb("hf"), zb("hi"), zb("hgate"), vec, vec, st],
        out_specs=[pl.BlockSpec((n, HG_V), lambda h: (0, h)), st],
        out_shape=[jax.ShapeDtypeStruct((n, BRANCH_W), F32),
                   jax.ShapeDtypeStruct((n, HG_HEADS, HG_K, HG_V), F32)],
        compiler_params=_cparams("parallel"),
    )(z, z, z, z, lb.reshape(1, BRANCH_W), gon.reshape(1, BRANCH_W), s0)


def _compress_consts(pe, w1, b1, w2):
    eye2 = jnp.eye(KV_HEADS, dtype=F32)
    w1r = w1.reshape(2, CMP_BLOCK, HEAD_DIM, HEAD_DIM)
    kron = lambda a: jnp.einsum('ij,ktdh->ktidjh', eye2, a).reshape(2, CMP_STRIDE, KV_W, KV_W)
    w1ab = jnp.concatenate([kron(w1r[:, :CMP_STRIDE]), kron(w1r[:, CMP_STRIDE:])], axis=-1).astype(BF16)
    pe8 = jnp.zeros((2, 8, CMP_BLOCK * HEAD_DIM), F32).at[:, 0].set(pe.reshape(2, -1)).astype(BF16)
    w1d = jnp.tile(w1, (1, 1, KV_HEADS)).astype(BF16)
    b1d = jnp.tile(b1, (1, KV_HEADS)).reshape(2, 1, KV_W).astype(F32)
    w2bd = jnp.einsum('ij,kdh->kidjh', eye2, w2).reshape(2, KV_W, KV_W).astype(BF16)
    return w1ab, pe8, w1d, b1d, w2bd


def _compress_body(load_chunk_rows, n_ch, w1ab_ref, pe_ref, w1d_ref, b1_ref, w2_ref, pb_scr, j):
    acc = jnp.zeros((n_ch, 2 * KV_W), F32)
    for t in range(CMP_STRIDE):
        acc = acc + load_chunk_rows(j, t, w1ab_ref)
    c0 = b1_ref[j] + jnp.dot(pe_ref[j], w1d_ref[j], preferred_element_type=F32)[0:1, :]
    pb_scr[0:n_ch, :] = acc[:, KV_W:]
    pb_scr[n_ch:n_ch + 8, :] = jnp.zeros((8, KV_W), F32)
    hid = acc[:, :KV_W] + pb_scr[pl.ds(1, n_ch), :] + c0
    hid = hid * jax.nn.sigmoid(hid)
    return jnp.dot(hid.astype(BF16), w2_ref[j], preferred_element_type=F32)


def _compress_kernel(ksrc_ref, vsrc_ref, w1ab_ref, pe_ref, w1d_ref, b1_ref, w2_ref, ck_ref, cv_ref, pb_scr):
    n_ch = ksrc_ref.shape[0] // CMP_STRIDE

    def load(j, t, w_ref):
        x = (ksrc_ref, vsrc_ref)[j][pl.ds(t, n_ch, stride=CMP_STRIDE), :]
        return jnp.dot(x.astype(BF16), w_ref[j, t], preferred_element_type=F32)

    for j, out_ref in enumerate((ck_ref, cv_ref)):
        out_ref[0] = _compress_body(load, n_ch, w1ab_ref, pe_ref, w1d_ref, b1_ref, w2_ref, pb_scr, j).astype(BF16)


def _compress(cmp32, consts, batch, seq):
    n_ch = seq // CMP_STRIDE
    full = lambda a: pl.BlockSpec(a.shape, lambda b: (0,) * a.ndim)
    out = pl.BlockSpec((1, n_ch, KV_W), lambda b: (b, 0, 0))
    return pl.pallas_call(
        _compress_kernel,
        name="compress",
        grid=(batch,),
        in_specs=[pl.BlockSpec((seq, KV_W), lambda b: (b, 0)), pl.BlockSpec((seq, KV_W), lambda b: (b, 1))]
        + [full(a) for a in consts],
        out_specs=[out, out],
        out_shape=[jax.ShapeDtypeStruct((batch, n_ch, KV_W), BF16)] * 2,
        scratch_shapes=[pltpu.VMEM((n_ch + 8, KV_W), F32)],
        compiler_params=_cparams("parallel"),
    )(cmp32, cmp32, *consts)


def _cover(n_rows, n_cmp, n_sel, n_cols):
    c_start = np.arange(n_rows)[:, None] * CMP_STRIDE
    s_start = np.arange(n_cols)[None, :] * SEL_BLOCK
    ov = np.clip(np.minimum(c_start + CMP_BLOCK, s_start + SEL_BLOCK) - np.maximum(c_start, s_start), 0, None)
    ov = ov.astype(np.float32) / CMP_BLOCK
    ov[n_cmp:, :] = 0.0
    ov[:, n_sel:] = 0.0
    return jnp.asarray(ov, BF16)


def _kth_largest(count_ge, max_lt, lo, hi, n_valid, k):
    top = hi
    top_ok = count_ge(top) >= k

    def halve(_, c):
        lo, hi = c
        mid = 0.5 * lo + 0.5 * hi
        ge = count_ge(mid) >= k
        return jnp.where(ge, mid, lo), jnp.where(ge, hi, mid)

    def check(lo, hi):
        t = max_lt(hi)
        done = (count_ge(t) >= k) | top_ok | (n_valid < k)
        return t, jnp.where(done, 1.0, 0.0)

    def refine(c):
        it, lo, hi, _, _ = c
        lo, hi = lax.fori_loop(0, 8, halve, (lo, hi))
        return (it + 1, lo, hi) + check(lo, hi)

    lo, hi = lax.fori_loop(0, 24, halve, (lo, hi))
    state = (jnp.int32(0), lo, hi) + check(lo, hi)
    _, _, _, t, _ = lax.while_loop(lambda c: (c[0] < 40) & (jnp.min(c[4]) < 0.5), refine, state)
    return jnp.where(n_valid < k, -jnp.inf, jnp.where(top_ok, top, t))


def _split2(x):
    hi = x.astype(BF16)
    return hi, (x - hi.astype(F32)).astype(BF16)


def _nt(a, b):
    return lax.dot_general(a, b, (((1,), (1,)), ((), ())), preferred_element_type=F32)


def _group_queries(q_ref, g):
    return jnp.concatenate([q_ref[:, h * LANES:(h + 1) * LANES] for h in range(g * GROUP, (g + 1) * GROUP)], axis=0)


def _chain_init(st_ref, c):
    st_ref[c, 0] = jnp.full((GROUP * QB, LANES), NEG, F32)
    st_ref[c, 1] = jnp.zeros((GROUP * QB, LANES), F32)
    st_ref[c, 2] = jnp.zeros((GROUP * QB, LANES), F32)


def _chain_step(st_ref, c, q4, k2, v2, bias2, ones):
    tile = lambda b: jnp.concatenate([b] * GROUP, axis=0)
    s = [_nt(q4, k2[j * QB:(j + 1) * QB]) + tile(bias2[j]) for j in range(2)]
    m_old = st_ref[c, 0]
    m_new = jnp.maximum(m_old, jnp.max(jnp.maximum(s[0], s[1]), axis=1, keepdims=True))
    alpha = jnp.exp(m_old - m_new)
    p = jnp.concatenate([jnp.exp(sj - m_new).astype(BF16) for sj in s], axis=1)
    st_ref[c, 0] = m_new
    st_ref[c, 1] = alpha * st_ref[c, 1] + jnp.dot(p, jnp.concatenate([ones, ones], axis=0),
                                                  preferred_element_type=F32)
    st_ref[c, 2] = alpha * st_ref[c, 2] + jnp.dot(p, v2, preferred_element_type=F32)


def _chain_result(st_ref, c, hh):
    rows = slice(hh * QB, (hh + 1) * QB)
    return st_ref[c, 2, rows, :] / st_ref[c, 1, rows, :]


def _pair_to_lanes(o_even, o_odd, h_even, lane):
    left = o_even if h_even // GROUP == 0 else pltpu.roll(o_even, HEAD_DIM, 1)
    right = o_odd if (h_even + 1) // GROUP == 1 else pltpu.roll(o_odd, HEAD_DIM, 1)
    return jnp.where(lane < HEAD_DIM, left, right)


def _topn_mask(v, n_cand, n_top, lane):
    rank = jnp.zeros(v.shape, F32)
    for i in range(n_cand):
        ci = v[:, i:i + 1]
        rank = rank + jnp.where((ci > v) | ((ci == v) & (lane > i)), 1.0, 0.0)
    return jnp.where((rank < n_top) & (lane < n_cand), 1.0, 0.0)


def _attn_prompt_kernel(qd_ref, qi_ref, qn_ref, iw_ref, nbg_ref,
                        kd_ref, vd_ref, ki_ref, ks_ref, vs_ref, kw_ref, vw_ref, ck_ref, cv_ref, cover_ref,
                        yb_ref, yc_ref, key_scr, db_scr, sb_scr, acc_scr, st_scr, *, n_top, n_sel):
    i = pl.program_id(1)
    nkb = i + 1
    row = lax.broadcasted_iota(I32, (QB, LANES), 0)
    lane = lax.broadcasted_iota(I32, (QB, LANES), 1)
    qpos = i * QB + row

    for h in range(IDX_HEADS):
        acc_scr[h] = jnp.broadcast_to(iw_ref[:, h:h + 1], (QB, LANES))

    def idx_pair(j, c):
        for kb in (2 * j, 2 * j + 1):
            kblk = ki_ref[pl.ds(pl.multiple_of(kb * QB, QB), QB), :]
            score = jnp.zeros((QB, LANES), F32)
            for h in range(IDX_HEADS):
                score = score + acc_scr[h] * jnp.maximum(_nt(qi_ref[:, h * LANES:(h + 1) * LANES], kblk), 0.0)
            key_scr[kb] = jnp.where(kb * QB + lane <= qpos, score, -jnp.inf)
        return c

    n_pairs = (nkb + 1) // 2
    lax.fori_loop(0, n_pairs, idx_pair, 0)

    def sweep(fn, init, reduce):
        acc = lax.fori_loop(0, nkb, lambda kb, a: fn(a, key_scr[kb]), jnp.full((QB, LANES), init, F32))
        return reduce(acc, axis=1, keepdims=True)

    count_ge = lambda x: sweep(lambda a, k: a + jnp.where(k >= x, 1.0, 0.0), 0.0, jnp.sum)
    max_lt = lambda x: sweep(lambda a, k: jnp.maximum(a, jnp.where(k < x, k, -jnp.inf)), -jnp.inf, jnp.max)
    lo = sweep(lambda a, k: jnp.minimum(a, jnp.where(k == -jnp.inf, jnp.inf, k)), jnp.inf, jnp.min)
    hi = sweep(jnp.maximum, -jnp.inf, jnp.max)
    n_valid = (qpos[:, 0:1] + 1).astype(F32)
    thr = _kth_largest(count_ge, max_lt, lo, hi, n_valid, n_top)
    need = n_top - sweep(lambda a, k: a + jnp.where(k > thr, 1.0, 0.0), 0.0, jnp.sum)
    upper = jnp.where(row <= lane, 1.0, 0.0).astype(BF16)

    def mask_blk(kb, before):
        key = key_scr[kb]
        eq = key == thr
        pref = jnp.dot(jnp.where(eq, 1.0, 0.0).astype(BF16), upper, preferred_element_type=F32) + before
        sel = ((key > thr) | (eq & (pref <= need))) & (kb * QB + lane <= qpos)
        db_scr[kb] = jnp.where(sel, 0.0, NEG)
        return before + jnp.sum(jnp.where(eq, 1.0, 0.0), axis=1, keepdims=True)

    lax.fori_loop(0, nkb, mask_blk, jnp.zeros((QB, 1), F32))

    gb = jax.nn.sigmoid(nbg_ref[...])
    gcol = lambda h, r: gb[:, 3 * h + r:3 * h + r + 1]
    ck, cv = ck_ref[0], cv_ref[0]
    n_ch = ck.shape[0]
    cl = lax.broadcasted_iota(I32, (QB, n_ch), 1)
    cvalid = cl * CMP_STRIDE + (CMP_BLOCK - 1) <= i * QB + lax.broadcasted_iota(I32, (QB, n_ch), 0)
    for g in range(KV_HEADS):
        imp = jnp.zeros((QB, n_ch), F32)
        for h in range(g * GROUP, (g + 1) * GROUP):
            s = jnp.where(cvalid, _nt(qn_ref[:, h * LANES:(h + 1) * LANES], ck), NEG)
            e = jnp.exp(s - jnp.max(s, axis=1, keepdims=True))
            p = jnp.where(cvalid, e / jnp.sum(e, axis=1, keepdims=True), 0.0)
            acc_scr[h] = gcol(h, 0) * jnp.dot(p.astype(BF16), cv, preferred_element_type=F32)
            imp = imp + p
        ih, il = _split2(imp)
        impj = (jnp.dot(ih, cover_ref[...], preferred_element_type=F32)
                + jnp.dot(il, cover_ref[...], preferred_element_type=F32))
        tb = qpos >> 6
        forced = (lane == 0) | (lane == tb) | (lane == tb - 1)
        v = jnp.where(lane * SEL_BLOCK <= qpos, jnp.where(forced, FORCE, impj), NEG)
        sel16 = _topn_mask(v, n_sel, min(SEL_TOPN, n_sel), lane).astype(BF16)

        def sb_blk(kb, c, g=g, sel16=sel16):
            expand = jnp.where(row == 2 * kb + (lane >> 6), 1.0, 0.0).astype(BF16)
            tok = jnp.dot(sel16, expand, preferred_element_type=F32)
            sb_scr[g, kb] = jnp.where((tok > 0.5) & (kb * QB + lane <= qpos), 0.0, NEG)
            return c

        lax.fori_loop(0, nkb, sb_blk, 0)

    ones = jnp.ones((LANES, LANES), BF16)
    for c in range(3 * KV_HEADS):
        _chain_init(st_scr, c)

    def pair_dsa_sel(j, c):
        r = pl.ds(pl.multiple_of(j * 2 * QB, 2 * QB), 2 * QB)
        second = 2 * j + 1 < nkb
        masked = lambda ref, idx: (ref[idx + (2 * j,)], jnp.where(second, ref[idx + (2 * j + 1,)], NEG))
        for g in range(KV_HEADS):
            _chain_step(st_scr, g, _group_queries(qd_ref, g), kd_ref[r, :], vd_ref[r, :], masked(db_scr, ()), ones)
            _chain_step(st_scr, KV_HEADS + g, _group_queries(qn_ref, g), ks_ref[r, :], vs_ref[r, :],
                        masked(sb_scr, (g,)), ones)
        return c

    def pair_win(j, c):
        r = pl.ds(pl.multiple_of(j * 2 * QB, 2 * QB), 2 * QB)
        kpos = [(2 * j + d) * QB + lane for d in range(2)]
        bias2 = [jnp.where((kp <= qpos) & (qpos - kp < WINDOW), 0.0, NEG) for kp in kpos]
        for g in range(KV_HEADS):
            _chain_step(st_scr, 2 * KV_HEADS + g, _group_queries(qn_ref, g), kw_ref[r, :], vw_ref[r, :], bias2, ones)
        return c

    lax.fori_loop(0, n_pairs, pair_dsa_sel, 0)
    lax.fori_loop(jnp.maximum(i - WINDOW // QB, 0) // 2, n_pairs, pair_win, 0)

    for j in range(N_HEADS // 2):
        cols = slice(j * LANES, (j + 1) * LANES)
        o_d, o_c = [], []
        for h in (2 * j, 2 * j + 1):
            g, hh = divmod(h, GROUP)
            o_d.append(_chain_result(st_scr, g, hh))
            o_c.append(acc_scr[h] + gcol(h, 1) * _chain_result(st_scr, KV_HEADS + g, hh)
                       + gcol(h, 2) * _chain_result(st_scr, 2 * KV_HEADS + g, hh))
        yb_ref[:, cols] = _pair_to_lanes(o_d[0], o_d[1], 2 * j, lane)
        yc_ref[:, cols] = _pair_to_lanes(o_c[0], o_c[1], 2 * j, lane)


def _attn_prompt(qd, qi, qn, iw, z, kv16, ck, cv, batch, seq):
    nq = seq // QB
    assert seq % (2 * QB) == 0
    n_ch = seq // CMP_STRIDE
    n_sel = seq // SEL_BLOCK
    n_cmp = (seq - CMP_BLOCK) // CMP_STRIDE + 1
    cover = _cover(n_ch, n_cmp, n_sel, LANES)
    qspec = pl.BlockSpec((QB, N_HEADS * LANES), lambda b, i: (b * nq + i, 0))
    zspec = lambda name, w: pl.BlockSpec((QB, w), lambda b, i, o=Z_OFF[name] // w: (b * nq + i, o))
    kvspec = lambda blk: pl.BlockSpec((seq, LANES), lambda b, i: (b, blk))
    cspec = pl.BlockSpec((1, n_ch, KV_W), lambda b, i: (b, 0, 0))
    yspec = pl.BlockSpec((QB, BRANCH_W), lambda b, i: (b * nq + i, 0))
    kern = functools.partial(_attn_prompt_kernel, n_top=min(DSA_TOPK, seq // 4), n_sel=n_sel)
    return pl.pallas_call(
        kern,
        name="attn_prompt",
        grid=(batch, nq),
        in_specs=[qspec, qspec, qspec, pl.BlockSpec((QB, LANES), lambda b, i: (b * nq + i, 0)),
                  zspec("nbg", LANES),
                  kvspec(KV_KD), kvspec(KV_VD), kvspec(KV_KI), kvspec(KV_KS), kvspec(KV_VS), kvspec(KV_KW),
                  kvspec(KV_VW), cspec, cspec, pl.BlockSpec(cover.shape, lambda b, i: (0, 0))],
        out_specs=[yspec, yspec],
        out_shape=[jax.ShapeDtypeStruct((batch * seq, BRANCH_W), F32)] * 2,
        scratch_shapes=[pltpu.VMEM((nq, QB, LANES), F32), pltpu.VMEM((nq, QB, LANES), F32),
                        pltpu.VMEM((KV_HEADS, nq, QB, LANES), F32), pltpu.VMEM((N_HEADS, QB, LANES), F32),
                        pltpu.VMEM((3 * KV_HEADS, 3, GROUP * QB, LANES), F32)],
        compiler_params=_cparams("parallel", "arbitrary"),
    )(qd, qi, qn, iw, z, kv16, kv16, kv16, kv16, kv16, kv16, kv16, ck, cv, cover)


R_DK, R_DV, R_IK, R_CK, R_CV, R_SK, R_SV = 0, 128, 256, 320, 448, 576, 704
ROW_GROUPS = ((R_IK, R_CK), (R_DK, R_IK), (R_CK, R_SK), (R_SK, CACHE_W))
G_IDX, G_DSA, G_CMP, G_SEL = range(4)


def _key_chunk(n_blocks):
    return QB * max(d for d in range(1, 17) if n_blocks % d == 0)


def _attn_sample_kernel(pt_ref, cache_ref, new_ref, qd_ref, qi_ref, qn_ref, iw_ref, nbg_ref, win_ref,
                        w1ab_ref, pe_ref, w1d_ref, b1_ref, w2_ref, cover_ref,
                        od_ref, oc_ref, buf, cmp_scr, pb_scr, tok_scr, pref_scr, sems, *, layer, past, n_top):
    b = pl.program_id(0)
    n_pages = past // PAGE_SIZE
    nk = past + QB
    kc = _key_chunk(nk // QB)
    n_ch = past // CMP_STRIDE
    n_sel = -(-(past + 1) // SEL_BLOCK)
    n_selp = cover_ref.shape[1]

    def group_copy(g, p):
        r0, r1 = ROW_GROUPS[g]
        cols = pl.ds(pl.multiple_of(p * PAGE_SIZE, PAGE_SIZE), PAGE_SIZE)
        return pltpu.make_async_copy(cache_ref.at[layer, pt_ref[b, p], pl.ds(r0, r1 - r0), :],
                                     buf.at[pl.ds(r0, r1 - r0), cols], sems.at[g])

    def each_page(fn):
        def body(p, c):
            fn(p)
            return c
        lax.fori_loop(0, n_pages, body, 0)

    for g in range(len(ROW_GROUPS)):
        each_page(lambda p, g=g: group_copy(g, p).start())
    buf[:, pl.ds(past, QB)] = jnp.zeros((CACHE_W, QB), F32)
    buf[:, pl.ds(past, 1)] = new_ref[0]
    arrived = lambda g: each_page(lambda p: group_copy(g, p).wait())

    kpos1 = lax.broadcasted_iota(I32, (1, nk), 1)
    in_range = kpos1 <= past
    lo_heads = lax.broadcasted_iota(I32, (N_HEADS, HEAD_DIM), 0) < GROUP

    def scores(q, r0):
        n = q.shape[1]
        return jnp.concatenate([jnp.dot(q, buf[r0:r0 + n, c:c + kc], preferred_element_type=F32)
                                for c in range(0, nk, kc)], axis=1)

    def weighted(p, r0):
        return sum(_nt(p[:, c:c + kc], buf[r0:r0 + KV_W, c:c + kc]) for c in range(0, nk, kc))

    def softmax(s):
        e = jnp.exp(s - jnp.max(s, axis=1, keepdims=True))
        return e / jnp.sum(e, axis=1, keepdims=True)

    compact = lambda o: jnp.where(lo_heads, o[:, :HEAD_DIM], o[:, HEAD_DIM:])

    arrived(G_IDX)
    sc = jnp.sum(iw_ref[0] * jnp.maximum(scores(qi_ref[0], R_IK), 0.0), axis=0, keepdims=True)
    key = jnp.where(in_range, sc, -jnp.inf)
    count = lambda pred: jnp.sum(jnp.where(pred, 1.0, 0.0), axis=1, keepdims=True)
    thr = _kth_largest(lambda x: count(key >= x),
                       lambda x: jnp.max(jnp.where(key < x, key, -jnp.inf), axis=1, keepdims=True),
                       jnp.min(jnp.where(in_range, sc, jnp.inf), axis=1, keepdims=True),
                       jnp.max(key, axis=1, keepdims=True), jnp.full((1, 1), past + 1.0, F32), n_top)
    need = n_top - count(key > thr)
    eq8 = jnp.where(jnp.broadcast_to(key == thr, (N_HEADS, nk)), 1.0, 0.0)
    upper = jnp.where(lax.broadcasted_iota(I32, (LANES, LANES), 0) <= lax.broadcasted_iota(I32, (LANES, LANES), 1),
                      1.0, 0.0).astype(BF16)
    before = jnp.zeros((N_HEADS, 1), F32)
    for c in range(0, nk, LANES):
        pref = jnp.dot(eq8[:, c:c + LANES].astype(BF16), upper, preferred_element_type=F32) + before
        pref_scr[:, c:c + LANES] = pref
        before = before + jnp.sum(eq8[:, c:c + LANES], axis=1, keepdims=True)
    sel = ((key > thr) | ((key == thr) & (pref_scr[0:1, :] <= need))) & in_range
    arrived(G_DSA)
    p = softmax(scores(qd_ref[0], R_DK) + jnp.where(sel, 0.0, NEG))
    od_ref[0] = compact(weighted(p, R_DV))

    arrived(G_CMP)

    def to_token_major(p):
        cols = pl.ds(pl.multiple_of(p * PAGE_SIZE, PAGE_SIZE), PAGE_SIZE)
        for j, r0 in enumerate((R_CK, R_CV)):
            cmp_scr[j, cols, :] = buf[r0:r0 + KV_W, cols].T

    each_page(to_token_major)

    def load(j, t, w_ref):
        x = cmp_scr[j, pl.ds(t, n_ch, stride=CMP_STRIDE), :]
        return jnp.dot(x.astype(BF16), w_ref[j, t], preferred_element_type=F32)

    ck = _compress_body(load, n_ch, w1ab_ref, pe_ref, w1d_ref, b1_ref, w2_ref, pb_scr, 0).astype(BF16)
    cv = _compress_body(load, n_ch, w1ab_ref, pe_ref, w1d_ref, b1_ref, w2_ref, pb_scr, 1).astype(BF16)

    gb = jax.nn.sigmoid(nbg_ref[0])
    qn = qn_ref[0]
    cvalid = lax.broadcasted_iota(I32, (N_HEADS, n_ch), 1) * CMP_STRIDE + (CMP_BLOCK - 1) <= past
    s = jnp.where(cvalid, _nt(qn.astype(BF16), ck), NEG)
    e = jnp.exp(s - jnp.max(s, axis=1, keepdims=True))
    pc = jnp.where(cvalid, e / jnp.sum(e, axis=1, keepdims=True), 0.0)
    o_cmp = compact(jnp.dot(pc.astype(BF16), cv, preferred_element_type=F32))
    imp = jnp.where(lax.broadcasted_iota(I32, (N_HEADS, n_ch), 0) < GROUP,
                    jnp.sum(pc[:GROUP], axis=0, keepdims=True), jnp.sum(pc[GROUP:], axis=0, keepdims=True))
    ih, il = _split2(imp)
    impj = (jnp.dot(ih, cover_ref[...], preferred_element_type=F32)
            + jnp.dot(il, cover_ref[...], preferred_element_type=F32))
    lane_s = lax.broadcasted_iota(I32, (N_HEADS, n_selp), 1)
    tb = past // SEL_BLOCK
    forced = (lane_s == 0) | (lane_s == tb) | (lane_s == tb - 1)
    v = jnp.where(lane_s * SEL_BLOCK <= past, jnp.where(forced, FORCE, impj), NEG)
    sel8 = _topn_mask(v, n_sel, min(SEL_TOPN, n_sel), lane_s)
    half = lax.broadcasted_iota(I32, (N_HEADS, LANES), 1) < SEL_BLOCK
    for c in range(nk // LANES):
        tok_scr[:, c * LANES:(c + 1) * LANES] = jnp.where(half, sel8[:, 2 * c:2 * c + 1], sel8[:, 2 * c + 1:2 * c + 2])

    arrived(G_SEL)
    p = softmax(jnp.where((tok_scr[...] > 0.5) & in_range, scores(qn, R_SK), NEG))
    o_sel = compact(weighted(p, R_SV))

    pw = softmax(_nt(qn, win_ref[0, :, 0:KV_W]))
    o_win = compact(jnp.dot(pw, win_ref[0, :, KV_W:2 * KV_W], preferred_element_type=F32))
    oc_ref[0] = gb[:, 0:1] * o_cmp + gb[:, 1:2] * o_sel + gb[:, 2:3] * o_win


def _attn_sample(page_table, cache_t, layer, new_rows, qd, qi, qn, iw, nbg, win, cmp_consts):
    nb, n_pages = page_table.shape
    past = n_pages * PAGE_SIZE
    nk = past + QB
    n_ch = past // CMP_STRIDE
    n_sel = -(-(past + 1) // SEL_BLOCK)
    n_selp = -(-n_sel // LANES) * LANES
    n_cmp = (past + 1 - CMP_BLOCK) // CMP_STRIDE + 1
    cover = _cover(n_ch, n_cmp, n_sel, n_selp)
    consts = tuple(cmp_consts) + (cover,)
    per_seq = lambda a: pl.BlockSpec((1,) + a.shape[1:], lambda b, pt: (b,) + (0,) * (a.ndim - 1))
    full = lambda a: pl.BlockSpec(a.shape, lambda b, pt: (0,) * a.ndim)
    args = (new_rows, qd, qi, qn, iw, nbg, win)
    kern = functools.partial(_attn_sample_kernel, layer=layer, past=past, n_top=min(DSA_TOPK, (past + 1) // 4))
    out = pl.BlockSpec((1, N_HEADS, HEAD_DIM), lambda b, pt: (b, 0, 0))
    return pl.pallas_call(
        kern,
        name="attn_sample",
        grid_spec=pltpu.PrefetchScalarGridSpec(
            num_scalar_prefetch=1,
            grid=(nb,),
            in_specs=[pl.BlockSpec(memory_space=pl.ANY)] + [per_seq(a) for a in args] + [full(a) for a in consts],
            out_specs=[out, out],
            scratch_shapes=[pltpu.VMEM((CACHE_W, nk), F32), pltpu.VMEM((2, past, KV_W), F32),
                            pltpu.VMEM((n_ch + 8, KV_W), F32),
                            pltpu.VMEM((N_HEADS, nk), F32), pltpu.VMEM((N_HEADS, nk), F32),
                            pltpu.SemaphoreType.DMA((len(ROW_GROUPS),))]),
        out_shape=[jax.ShapeDtypeStruct((nb, N_HEADS, HEAD_DIM), F32)] * 2,
        compiler_params=_cparams("arbitrary"),
    )(page_table, cache_t, *args, *consts)


def _merge_kernel(x_ref, ya_ref, yb_ref, yc_ref, dg_ref, ng_ref, ga_ref, gb_ref, gc_ref, wb_ref, wo_ref, y_ref):
    silu = lambda v: v * jax.nn.sigmoid(v)
    ys = (ya_ref[...], yb_ref[...] * silu(dg_ref[...]), yc_ref[...] * silu(ng_ref[...]))
    m = None
    for n, (y, g) in enumerate(zip(ys, (ga_ref, gb_ref, gc_ref))):
        pr = jax.nn.sigmoid(g[...]) * jnp.dot(y.astype(BF16), wb_ref[n], preferred_element_type=F32)
        m = pr if m is None else m + pr
    y_ref[...] = x_ref[...] + jnp.dot(m.astype(BF16), wo_ref[...], preferred_element_type=F32)


def _merge(x2, ya, yb, yc, z, wb16, wo16):
    t, d = x2.shape
    tm = min(t, 512)
    yspec = pl.BlockSpec((tm, BRANCH_W), lambda i: (i, 0))
    gspec = lambda name, w=d: pl.BlockSpec((tm, w), lambda i, o=Z_OFF[name] // w: (i, o))
    return pl.pallas_call(
        _merge_kernel,
        name="merge",
        grid=(t // tm,),
        in_specs=[pl.BlockSpec((tm, d), lambda i: (i, 0)), yspec, yspec, yspec,
                  gspec("dgate", BRANCH_W), gspec("ngate", BRANCH_W), gspec("ma"), gspec("mb"), gspec("mc"),
                  pl.BlockSpec(wb16.shape, lambda i: (0, 0, 0)), pl.BlockSpec(wo16.shape, lambda i: (0, 0))],
        out_specs=pl.BlockSpec((tm, d), lambda i: (i, 0)),
        out_shape=jax.ShapeDtypeStruct((t, d), F32),
        compiler_params=_cparams("parallel"),
    )(x2, ya, yb, yc, z, z, z, z, z, wb16, wo16)


def _layer_prompt(x, lb, norm_g, w16, gon, prep_consts, cmp_consts, wb16, wo16):
    batch, seq, d = x.shape
    x2 = x.reshape(batch * seq, d)
    z = _proj(x2, norm_g, w16)
    tabs = _rope_tables(jnp.arange(seq, dtype=I32))
    rows, wrows, cmp32, qd, qi, qn, iw, kv16 = _prep(z, tabs, *prep_consts, min(seq, 256))
    ya, s_new = _hgrn_prompt(z, lb, gon, batch, seq)
    ck, cv = _compress(cmp32, cmp_consts, batch, seq)
    yb, yc = _attn_prompt(qd, qi, qn, iw, z, kv16, ck, cv, batch, seq)
    y = _merge(x2, ya, yb, yc, z, wb16, wo16).reshape(batch, seq, d)
    wrows = wrows.reshape(batch, seq, WIN_W)
    return y, rows.reshape(batch, seq, CACHE_W), wrows[:, -min(WINDOW, seq):], s_new


def _layer_sample(x, lb, norm_g, w16, gon, prep_consts, cmp_consts, wb16, wo16,
                  page_table, cache_kv, layer, win_rows, s0):
    nb, one, d = x.shape
    assert one == 1 and win_rows.shape[1] == WINDOW
    past = page_table.shape[1] * PAGE_SIZE
    x2 = x.reshape(nb, d)
    z = _proj(x2, norm_g, w16)
    tabs = _rope_tables(jnp.full((nb,), past, I32))
    rows, wrows, _, qd, qi, qn, iw, _ = _prep(z, tabs, *prep_consts, nb)
    ya, s_new = _hgrn_sample(z, lb, gon, s0)
    win = jnp.concatenate([win_rows[:, 1:], wrows[:, None, :]], axis=1)
    heads = lambda q: q.astype(F32).reshape(nb, N_HEADS, LANES)
    nbg = z[:, Z_OFF["nbg"]:Z_OFF["nbg"] + 3 * N_HEADS].reshape(nb, N_HEADS, 3)
    od, oc = _attn_sample(page_table, cache_kv, layer, rows.reshape(nb, CACHE_W, 1), heads(qd),
                          heads(qi)[:, :, :IDX_DIM], heads(qn), iw[:, :IDX_HEADS].reshape(nb, IDX_HEADS, 1), nbg, win,
                          cmp_consts)
    y = _merge(x2, ya, od.reshape(nb, BRANCH_W), oc.reshape(nb, BRANCH_W), z, wb16, wo16)
    return y.reshape(nb, 1, d), rows.reshape(nb, 1, CACHE_W), win, s_new


def kernel(x_prompt, x_sample, cache_kv, page_table, state_win, state_hgrn, norm_g, w_in,
           hgrn_lb_logits, hgrn_onorm_g, dsa_qk_norm_g, dsa_idx_k_norm_g, nsa_qk_norm_g,
           nsa_cmp_pe, nsa_cmp_w1, nsa_cmp_b1, nsa_cmp_w2, w_branch, w_out):
    lb_soft = jax.nn.softmax(hgrn_lb_logits.astype(F32), axis=0)
    lbs = jnp.cumsum(lb_soft, axis=0) - lb_soft[0]
    xp, xs = x_prompt, x_sample
    cache_t = jnp.swapaxes(cache_kv, 2, 3)
    outs = [[] for _ in range(6)]
    for l in range(w_in.shape[0]):
        shared = (lbs[l], norm_g[l], _perm_w_in(w_in[l]), hgrn_onorm_g[l],
                  _prep_consts(dsa_qk_norm_g[l], dsa_idx_k_norm_g[l], nsa_qk_norm_g[l]),
                  _compress_consts(nsa_cmp_pe[l], nsa_cmp_w1[l], nsa_cmp_b1[l], nsa_cmp_w2[l]),
                  w_branch[l].astype(BF16), w_out[l].astype(BF16))
        xs, *s_out = _layer_sample(xs, *shared, page_table, cache_t, l, state_win[l], state_hgrn[l])
        xp, *p_out = _layer_prompt(xp, *shared)
        for acc, o in zip(outs, p_out + s_out):
            acc.append(o)
    return (xp, xs) + tuple(jnp.stack(o) for o in outs)
```
